```python
import jax
import jax.numpy as jnp
from jax import lax
import numpy as np

D_MODEL = 2048
BATCH = 16
SEQ = 256
DEPTH = 4
DEC_BATCH = 4
DEC_SEQ = 4096
PAST_LEN = 256

GRID_W = 64
HEAD_DIM = 128
ROPE_THETA = 10000.0
QBLOCK = 128
WINDOW = 128
RMS_EPS = 1e-6
NEG_INF = -1e30

A_GROUPS = 4
A_GROUP_DIM = D_MODEL // 8
A_WIDTH = A_GROUPS * A_GROUP_DIM
B_HEADS = (D_MODEL - A_WIDTH) // HEAD_DIM
B_KV_HEADS = B_HEADS // 4
C_HEADS = D_MODEL // HEAD_DIM
C_KV_HEADS = C_HEADS // 4
EVEN_IN = A_WIDTH + (B_HEADS + 2 * B_KV_HEADS) * HEAD_DIM
EVEN_OUT = A_WIDTH + B_HEADS * HEAD_DIM
ODD_IN = (C_HEADS + 2 * C_KV_HEADS) * HEAD_DIM
ODD_OUT = C_HEADS * HEAD_DIM
N_EVEN = (DEPTH + 1) // 2
N_ODD = DEPTH // 2
N_GROUPS = 8
EXPERTS_PER_GROUP = 8
N_EXPERTS = N_GROUPS * EXPERTS_PER_GROUP
TOP_K = 2
D_EXPERT = D_MODEL // 4
MOE_BLOCK = 128

kernel_name = 'hybrid_fourier_swa_axial_hmoe_diffusion_step'


def rms_norm(x, g):
    xf = x.astype(jnp.float32)
    y = xf * lax.rsqrt(jnp.mean(xf * xf, axis=-1, keepdims=True) + RMS_EPS)
    return (y * g.astype(jnp.float32)).astype(x.dtype)


def axial_rope_tables(n_tok):
    n_rows = n_tok // GRID_W
    row = jnp.repeat(jnp.arange(n_rows, dtype=jnp.float32), GRID_W)
    col = jnp.tile(jnp.arange(GRID_W, dtype=jnp.float32), n_rows)
    axis_dim = HEAD_DIM // 2
    inv_freq = ROPE_THETA ** (-jnp.arange(0, axis_dim, 2, dtype=jnp.float32) / axis_dim)
    ang = jnp.stack([row[:, None] * inv_freq, col[:, None] * inv_freq], axis=1)
    return jnp.cos(ang), jnp.sin(ang)


def apply_axial_rope(x, cos, sin):
    shp = x.shape
    xf = x.astype(jnp.float32).reshape(shp[:-1] + (2, 2, HEAD_DIM // 4))
    x1, x2 = xf[..., 0, :], xf[..., 1, :]
    c = cos[None, :, None]
    s = sin[None, :, None]
    out = jnp.stack([x1 * c - x2 * s, x2 * c + x1 * s], axis=-2)
    return out.reshape(shp).astype(x.dtype)


def fourier_mix(u):
    b, s, _ = u.shape
    uf = u.astype(jnp.float32).reshape(b, s, A_GROUPS, A_GROUP_DIM)
    y = jnp.fft.fft2(uf, axes=(1, 3), norm='ortho').real
    return y.reshape(b, s, A_WIDTH).astype(u.dtype)


def context_attention(q, k, v, sink):
    s = jnp.einsum('bqhgd,bkhd->bhgqk', q, k, preferred_element_type=jnp.float32)
    n_k = k.shape[1]
    if sink is not None:
        sk = jnp.broadcast_to(sink.astype(jnp.float32)[None, :, :, None, None], s.shape[:-1] + (1,))
        s = jnp.concatenate([s, sk], axis=-1)
    p = jax.nn.softmax(s, axis=-1)[..., :n_k].astype(v.dtype)
    o = jnp.einsum('bhgqk,bkhd->bqhgd', p, v)
    return o.reshape(o.shape[0], o.shape[1], -1)


def window_attention(q, k, v, kc, vc, sink):
    b, n_lat, kvh, g, dh = q.shape
    nb = n_lat // QBLOCK
    pad = ((0, 0), (WINDOW, WINDOW), (0, 0), (0, 0))
    kp = jnp.pad(k, pad)
    vp = jnp.pad(v, pad)
    kb = jnp.concatenate([kp[:, i * QBLOCK:i * QBLOCK + n_lat].reshape(b, nb, QBLOCK, kvh, dh) for i in range(3)], axis=2)
    vb = jnp.concatenate([vp[:, i * QBLOCK:i * QBLOCK + n_lat].reshape(b, nb, QBLOCK, kvh, dh) for i in range(3)], axis=2)
    qb = q.reshape(b, nb, QBLOCK, kvh, g, dh)
    s_w = jnp.einsum('bnqhgd,bnjhd->bnhgqj', qb, kb, preferred_element_type=jnp.float32)
    qpos = jnp.arange(nb)[:, None, None] * QBLOCK + jnp.arange(QBLOCK)[None, :, None]
    kpos = jnp.arange(nb)[:, None, None] * QBLOCK - WINDOW + jnp.arange(3 * QBLOCK)[None, None, :]
    valid = (jnp.abs(kpos - qpos) <= WINDOW) & (kpos >= 0) & (kpos < n_lat)
    s_w = jnp.where(valid[None, :, None, None], s_w, NEG_INF)
    s_c = jnp.einsum('bnqhgd,bchd->bnhgqc', qb, kc, preferred_element_type=jnp.float32)
    sk = jnp.broadcast_to(sink.astype(jnp.float32)[None, None, :, :, None, None], s_w.shape[:-1] + (1,))
    p = jax.nn.softmax(jnp.concatenate([s_w, s_c, sk], axis=-1), axis=-1)
    n_w = 3 * QBLOCK
    n_c = kc.shape[1]
    o = (jnp.einsum('bnhgqj,bnjhd->bnqhgd', p[..., :n_w].astype(v.dtype), vb)
         + jnp.einsum('bnhgqc,bchd->bnqhgd', p[..., n_w:n_w + n_c].astype(vc.dtype), vc))
    return o.reshape(b, n_lat, kvh * g * dh)


def dense_block_attention(q, k, v, kc, vc):
    b, n_lat, kvh, g, dh = q.shape
    nb = n_lat // QBLOCK
    qb = jnp.moveaxis(q.reshape(b, nb, QBLOCK, kvh, g, dh), 1, 0)

    def one_block(qi):
        s_l = jnp.einsum('bqhgd,bkhd->bhgqk', qi, k, preferred_element_type=jnp.float32)
        s_c = jnp.einsum('bqhgd,bchd->bhgqc', qi, kc, preferred_element_type=jnp.float32)
        p = jax.nn.softmax(jnp.concatenate([s_l, s_c], axis=-1), axis=-1).astype(v.dtype)
        return (jnp.einsum('bhgqk,bkhd->bqhgd', p[..., :n_lat], v)
                + jnp.einsum('bhgqc,bchd->bqhgd', p[..., n_lat:], vc))

    o = lax.map(one_block, qb)
    return jnp.moveaxis(o, 0, 1).reshape(b, n_lat, kvh * g * dh)


def even_mixer(h, w_in, w_out, qn, kn, sink, rope, ctx_kv):
    b, s, _ = h.shape
    proj = jnp.dot(h, w_in)
    nq = B_HEADS * HEAD_DIM
    nkv = B_KV_HEADS * HEAD_DIM
    a = proj[..., :A_WIDTH]
    q = rms_norm(proj[..., A_WIDTH:A_WIDTH + nq].reshape(b, s, B_HEADS, HEAD_DIM), qn)
    k = rms_norm(proj[..., A_WIDTH + nq:A_WIDTH + nq + nkv].reshape(b, s, B_KV_HEADS, HEAD_DIM), kn)
    v = proj[..., A_WIDTH + nq + nkv:].reshape(b, s, B_KV_HEADS, HEAD_DIM)
    grp = B_HEADS // B_KV_HEADS
    sink_g = sink.reshape(B_KV_HEADS, grp)
    if ctx_kv is None:
        qg = (q * HEAD_DIM ** -0.5).reshape(b, s, B_KV_HEADS, grp, HEAD_DIM)
        att = context_attention(qg, k, v, sink_g)
        new_kv = (k, v)
    else:
        cos, sin = rope
        q = apply_axial_rope(q, cos, sin)
        k = apply_axial_rope(k, cos, sin)
        qg = (q * HEAD_DIM ** -0.5).reshape(b, s, B_KV_HEADS, grp, HEAD_DIM)
        att = window_attention(qg, k, v, ctx_kv[0], ctx_kv[1], sink_g)
        new_kv = None
    out = jnp.dot(jnp.concatenate([fourier_mix(a), att], axis=-1), w_out)
    return out, new_kv


def odd_mixer(h, w_in, w_out, qn, kn, rope, ctx_kv):
    b, s, _ = h.shape
    proj = jnp.dot(h, w_in)
    nq = C_HEADS * HEAD_DIM
    nkv = C_KV_HEADS * HEAD_DIM
    q = rms_norm(proj[..., :nq].reshape(b, s, C_HEADS, HEAD_DIM), qn)
    k = rms_norm(proj[..., nq:nq + nkv].reshape(b, s, C_KV_HEADS, HEAD_DIM), kn)
    v = proj[..., nq + nkv:].reshape(b, s, C_KV_HEADS, HEAD_DIM)
    grp = C_HEADS // C_KV_HEADS
    if ctx_kv is None:
        qg = (q * HEAD_DIM ** -0.5).reshape(b, s, C_KV_HEADS, grp, HEAD_DIM)
        att = context_attention(qg, k, v, None)
        new_kv = (k, v)
    else:
        cos, sin = rope
        q = apply_axial_rope(q, cos, sin)
        k = apply_axial_rope(k, cos, sin)
        qg = (q * HEAD_DIM ** -0.5).reshape(b, s, C_KV_HEADS, grp, HEAD_DIM)
        att = dense_block_attention(qg, k, v, ctx_kv[0], ctx_kv[1])
        new_kv = None
    return jnp.dot(att, w_out), new_kv


def hier_moe(x, w_group, b_group, w_route, b_route, w_gate, w_up, w_down):
    n, d = x.shape
    g_logit = jnp.dot(x, w_group, preferred_element_type=jnp.float32) + b_group.astype(jnp.float32)
    g_prob = jax.nn.softmax(g_logit, axis=-1)
    g_idx = jnp.argmax(g_logit, axis=-1).astype(jnp.int32)
    g_w = jnp.take_along_axis(g_prob, g_idx[:, None], axis=-1)
    e_logit = (jnp.dot(x, w_route, preferred_element_type=jnp.float32)
               + b_route.astype(jnp.float32)).reshape(n, N_GROUPS, EXPERTS_PER_GROUP)
    e_logit = e_logit[jnp.arange(n), g_idx]
    top_p, top_i = lax.top_k(jax.nn.softmax(e_logit, axis=-1), TOP_K)
    wts = (g_w * top_p / jnp.sum(top_p, axis=-1, keepdims=True)).reshape(-1)
    eid = (g_idx[:, None] * EXPERTS_PER_GROUP + top_i).reshape(-1).astype(jnp.int32)
    n_assign = n * TOP_K
    order = jnp.argsort(eid)
    tok = order // TOP_K
    e_sorted = eid[order]
    sizes = jnp.bincount(eid, length=N_EXPERTS).astype(jnp.int32)
    padded = ((sizes + MOE_BLOCK - 1) // MOE_BLOCK) * MOE_BLOCK
    start = jnp.cumsum(sizes) - sizes
    pad_end = jnp.cumsum(padded)
    pad_start = pad_end - padded
    dest = pad_start[e_sorted] + (jnp.arange(n_assign, dtype=jnp.int32) - start[e_sorted])
    n_blocks = -(-(n_assign + N_EXPERTS * (MOE_BLOCK - 1)) // MOE_BLOCK)
    buf = jnp.zeros((n_blocks * MOE_BLOCK, d), x.dtype).at[dest].set(x[tok])
    blk_start = jnp.arange(n_blocks, dtype=jnp.int32) * MOE_BLOCK
    blk_expert = jnp.minimum(jnp.searchsorted(pad_end, blk_start, side='right'), N_EXPERTS - 1)

    def expert_block(args):
        xb, e = args
        hid = jax.nn.silu(jnp.dot(xb, w_gate[e])) * jnp.dot(xb, w_up[e])
        return jnp.dot(hid, w_down[e])

    out = lax.map(expert_block, (buf.reshape(n_blocks, MOE_BLOCK, d), blk_expert)).reshape(-1, d)
    ys = out[dest] * wts[order][:, None].astype(x.dtype)
    return jnp.zeros_like(x).at[tok].add(ys)


def setup_inputs(seed: int = 0) -> dict:
    key = jax.random.key(seed)
    ks = jax.random.split(key, 32)

    def nrm(k, shape, scale):
        return jax.random.normal(k, shape, jnp.float32) * scale

    def gain(k, shape):
        return 1.0 + 0.05 * jax.random.normal(k, shape, jnp.float32)

    d = D_MODEL
    return {
        'x_prompt': nrm(ks[0], (BATCH, SEQ, d), 1.0),
        'x_sample': nrm(ks[1], (DEC_BATCH, DEC_SEQ, d), 1.0),
        'cache_b_k': nrm(ks[2], (DEC_BATCH, N_EVEN, PAST_LEN, B_KV_HEADS, HEAD_DIM), 1.0),
        'cache_b_v': nrm(ks[3], (DEC_BATCH, N_EVEN, PAST_LEN, B_KV_HEADS, HEAD_DIM), 1.0),
        'cache_c_k': nrm(ks[4], (DEC_BATCH, N_ODD, PAST_LEN, C_KV_HEADS, HEAD_DIM), 1.0),
        'cache_c_v': nrm(ks[5], (DEC_BATCH, N_ODD, PAST_LEN, C_KV_HEADS, HEAD_DIM), 1.0),
        'c': nrm(ks[6], (DEC_BATCH, d), 1.0),
        'c_ctx': nrm(ks[7], (d,), 1.0),
        'w_mod': nrm(ks[8], (DEPTH, d, 6 * d), 0.5 * d ** -0.5),
        'b_mod': nrm(ks[9], (DEPTH, 6 * d), 0.01),
        'norm_mix': gain(ks[10], (DEPTH, d)),
        'norm_ffn': gain(ks[11], (DEPTH, d)),
        'w_in_even': nrm(ks[12], (N_EVEN, d, EVEN_IN), d ** -0.5),
        'w_out_even': nrm(ks[13], (N_EVEN, EVEN_OUT, d), EVEN_OUT ** -0.5),
        'q_norm_b': gain(ks[14], (N_EVEN, HEAD_DIM)),
        'k_norm_b': gain(ks[15], (N_EVEN, HEAD_DIM)),
        'sink_b': nrm(ks[16], (N_EVEN, B_HEADS), 1.0),
        'w_in_odd': nrm(ks[17], (N_ODD, d, ODD_IN), d ** -0.5),
        'w_out_odd': nrm(ks[18], (N_ODD, ODD_OUT, d), ODD_OUT ** -0.5),
        'q_norm_c': gain(ks[19], (N_ODD, HEAD_DIM)),
        'k_norm_c': gain(ks[20], (N_ODD, HEAD_DIM)),
        'w_group': nrm(ks[21], (DEPTH, d, N_GROUPS), d ** -0.5),
        'b_group': nrm(ks[22], (DEPTH, N_GROUPS), 0.01),
        'w_route': nrm(ks[23], (DEPTH, d, N_EXPERTS), d ** -0.5),
        'b_route': nrm(ks[24], (DEPTH, N_EXPERTS), 0.01),
        'w_gate': nrm(ks[25], (DEPTH, N_EXPERTS, d, D_EXPERT), d ** -0.5),
        'w_up': nrm(ks[26], (DEPTH, N_EXPERTS, d, D_EXPERT), d ** -0.5),
        'w_down': nrm(ks[27], (DEPTH, N_EXPERTS, D_EXPERT, d), D_EXPERT ** -0.5),
    }


def reference(x_prompt, x_sample, cache_b_k, cache_b_v, cache_c_k, cache_c_v, c, c_ctx,
              w_mod, b_mod, norm_mix, norm_ffn, w_in_even, w_out_even, q_norm_b, k_norm_b, sink_b,
              w_in_odd, w_out_odd, q_norm_c, k_norm_c, w_group, b_group, w_route, b_route,
              w_gate, w_up, w_down):

    def modulation(l, cvec):
        m = (jnp.dot(jax.nn.silu(cvec), w_mod[l]) + b_mod[l]).reshape(cvec.shape[0], 6, D_MODEL)
        return [m[:, i, None, :] for i in range(6)]

    def layer(l, x, cvec, rope, ctx_kv):
        sh1, sc1, g1, sh2, sc2, g2 = modulation(l, cvec)
        h = rms_norm(x, norm_mix[l]) * (1 + sc1) + sh1
        i = l // 2
        if l % 2 == 0:
            mix, kv = even_mixer(h, w_in_even[i], w_out_even[i], q_norm_b[i], k_norm_b[i], sink_b[i], rope, ctx_kv)
        else:
            mix, kv = odd_mixer(h, w_in_odd[i], w_out_odd[i], q_norm_c[i], k_norm_c[i], rope, ctx_kv)
        x = x + g1 * mix
        h = rms_norm(x, norm_ffn[l]) * (1 + sc2) + sh2
        b, s, d = h.shape
        ff = hier_moe(h.reshape(b * s, d), w_group[l], b_group[l], w_route[l], b_route[l],
                      w_gate[l], w_up[l], w_down[l]).reshape(b, s, d)
        return x + g2 * ff, kv

    ctx = x_prompt
    c_ctx_vec = c_ctx[None, :]
    b_keys, b_vals, c_keys, c_vals = [], [], [], []
    for l in range(DEPTH):
        ctx, (k_l, v_l) = layer(l, ctx, c_ctx_vec, None, None)
        if l % 2 == 0:
            b_keys.append(k_l)
            b_vals.append(v_l)
        else:
            c_keys.append(k_l)
            c_vals.append(v_l)
    y_prompt = ctx
    new_b_k = jnp.stack(b_keys, axis=1)
    new_b_v = jnp.stack(b_vals, axis=1)
    new_c_k = jnp.stack(c_keys, axis=1)
    new_c_v = jnp.stack(c_vals, axis=1)

    rope = axial_rope_tables(x_sample.shape[1])
    lat = x_sample
    for l in range(DEPTH):
        i = l // 2
        if l % 2 == 0:
            ctx_kv = (cache_b_k[:, i], cache_b_v[:, i])
        else:
            ctx_kv = (cache_c_k[:, i], cache_c_v[:, i])
        lat, _ = layer(l, lat, c, rope, ctx_kv)
    y_sample = lat

    return (y_prompt, y_sample, new_b_k, new_b_v, new_c_k, new_c_v)
```

```python
import functools
import math

import jax
import jax.numpy as jnp
from jax import lax
from jax.experimental import pallas as pl
from jax.experimental.pallas import tpu as pltpu

F32 = jnp.float32
BF16 = jnp.bfloat16

HEAD_DIM = 128
GRID_W = 64
ROPE_THETA = 10000.0
WINDOW = 128
RMS_EPS = 1e-6
NEG_INF = -1e30
A_GROUPS = 4
N_GROUPS = 8
TOP_K = 2
GQA_GROUP = 4
MOD_ROWS = 8
LOGIT_LANES = 128
ROW_TILE = 512
MOE_ROWS = 256
MIB = 1024 * 1024


def _cparams(sem, vmem_mib):
    return pltpu.CompilerParams(dimension_semantics=sem, vmem_limit_bytes=vmem_mib * MIB)


def _silu(x):
    return x / (1.0 + jnp.exp(-x))


def _dot(a, b):
    return jnp.dot(a, b, preferred_element_type=F32)


def _dot_t(a, b):
    return lax.dot_general(a, b, (((1,), (1,)), ((), ())), preferred_element_type=F32)


def _mod_kernel(c_ref, w_ref, b_ref, o_ref):
    s = _silu(c_ref[...]).astype(BF16)
    o_ref[0] = _dot(s, w_ref[0].astype(BF16)) + b_ref[0]


def _modulation(cvec, w_mod, b_mod):
    depth, d, n6 = w_mod.shape
    tn = 1024
    return pl.pallas_call(
        _mod_kernel,
        grid=(depth, n6 // tn),
        in_specs=[pl.BlockSpec((MOD_ROWS, d), lambda l, j: (0, 0)),
                  pl.BlockSpec((1, d, tn), lambda l, j: (l, 0, j)),
                  pl.BlockSpec((1, 1, tn), lambda l, j: (l, 0, j))],
        out_specs=pl.BlockSpec((1, MOD_ROWS, tn), lambda l, j: (l, 0, j)),
        out_shape=jax.ShapeDtypeStruct((depth, MOD_ROWS, n6), F32),
        compiler_params=_cparams(("arbitrary", "arbitrary"), 40),
        name="modulation",
    )(cvec, w_mod, b_mod.reshape(depth, 1, n6))


class _Rows:
    def __init__(self, n_ctx_seq, s_ctx, n_lat_seq, s_lat):
        self.n_ctx_seq, self.s_ctx, self.n_lat_seq, self.s_lat = n_ctx_seq, s_ctx, n_lat_seq, s_lat
        self.tc = n_ctx_seq * s_ctx
        self.tl = n_lat_seq * s_lat
        self.t = self.tc + self.tl

    def mod_row(self, i, tm):
        nct = self.tc // tm
        per = self.s_lat // tm
        return jnp.where(i < nct, 0, 1 + (i - nct) // per)


def _norm_mod(x, g, scale, shift):
    ms = jnp.mean(x * x, axis=-1, keepdims=True)
    y = x * lax.rsqrt(ms + RMS_EPS) * g
    return y * (1.0 + scale) + shift


def _in_proj_kernel(x_ref, mod_ref, g_ref, w_ref, o_ref, h_ref):
    @pl.when(pl.program_id(1) == 0)
    def _():
        h = _norm_mod(x_ref[...], g_ref[...], mod_ref[0, 1:2, :], mod_ref[0, 0:1, :])
        h_ref[...] = h.astype(BF16)

    o_ref[...] = _dot(h_ref[...], w_ref[...])


def _in_proj(rows, x, mod_l, g, w_bf16):
    t, d = x.shape
    n = w_bf16.shape[1]
    tm, tn = ROW_TILE, 512
    return pl.pallas_call(
        _in_proj_kernel,
        grid=(t // tm, n // tn),
        in_specs=[pl.BlockSpec((tm, d), lambda i, j: (i, 0)),
                  pl.BlockSpec((1, 6, d), lambda i, j: (rows.mod_row(i, tm), 0, 0)),
                  pl.BlockSpec((1, d), lambda i, j: (0, 0)),
                  pl.BlockSpec((d, tn), lambda i, j: (0, j))],
        out_specs=pl.BlockSpec((tm, tn), lambda i, j: (i, j)),
        out_shape=jax.ShapeDtypeStruct((t, n), F32),
        scratch_shapes=[pltpu.VMEM((tm, d), BF16)],
        compiler_params=_cparams(("arbitrary", "arbitrary"), 40),
        name="in_proj",
    )(x, mod_l, g.reshape(1, d), w_bf16)


def _swap32(y):
    lane = lax.broadcasted_iota(jnp.int32, y.shape, 1)
    return jnp.where((lane & 63) < 32, pltpu.roll(y, 96, 1), pltpu.roll(y, 32, 1))


def _prep_kernel(*refs, nq, nkv, rope, emit_f32):
    q_ref, k_ref, v_ref, qn_ref, kn_ref = refs[:5]
    pos = 5
    if rope:
        cos_ref, sin_ref = refs[pos:pos + 2]
        pos += 2
    qo_ref, ko_ref, vo_ref = refs[pos:pos + 3]
    pos += 3
    if emit_f32:
        kf_ref, vf_ref = refs[pos:pos + 2]

    def head_norm(x, gain):
        ms = jnp.mean(x * x, axis=-1, keepdims=True)
        y = x * lax.rsqrt(ms + RMS_EPS) * gain
        return y

    def rot(y):
        if not rope:
            return y
        return y * cos_ref[...] + _swap32(y) * sin_ref[...]

    q_scale = HEAD_DIM ** -0.5
    for h in range(nq):
        sl = slice(h * HEAD_DIM, (h + 1) * HEAD_DIM)
        y = rot(head_norm(q_ref[:, sl], qn_ref[...]))
        qo_ref[h] = (y * q_scale).astype(BF16)
    for h in range(nkv):
        sl = slice(h * HEAD_DIM, (h + 1) * HEAD_DIM)
        y = head_norm(k_ref[:, sl], kn_ref[...])
        v = v_ref[:, sl]
        if emit_f32:
            kf_ref[:, sl] = y
            vf_ref[:, sl] = v
        ko_ref[h] = rot(y).astype(BF16)
        vo_ref[h] = v.astype(BF16)


def _prep(proj, col0, nq, nkv, qn, kn, row0, n_rows, rope_tabs, emit_f32, t_total, prev):
    tm = 256
    rope = rope_tabs is not None
    r0 = row0 // tm
    qw, kw = nq * HEAD_DIM, nkv * HEAD_DIM
    qb, kb, vb = col0 // qw, (col0 + qw) // kw, (col0 + qw + kw) // kw
    assert col0 % qw == 0 and (col0 + qw) % kw == 0
    in_specs = [pl.BlockSpec((tm, qw), lambda i: (r0 + i, qb)),
                pl.BlockSpec((tm, kw), lambda i: (r0 + i, kb)),
                pl.BlockSpec((tm, kw), lambda i: (r0 + i, vb)),
                pl.BlockSpec((1, HEAD_DIM), lambda i: (0, 0)),
                pl.BlockSpec((1, HEAD_DIM), lambda i: (0, 0))]
    args = [proj, proj, proj, qn.reshape(1, HEAD_DIM), kn.reshape(1, HEAD_DIM)]
    if rope:
        s_lat = rope_tabs[0].shape[0]
        per = s_lat // tm
        in_specs += [pl.BlockSpec((tm, HEAD_DIM), lambda i: (i % per, 0)),
                     pl.BlockSpec((tm, HEAD_DIM), lambda i: (i % per, 0))]
        args += list(rope_tabs)
    out_specs = [pl.BlockSpec((nq, tm, HEAD_DIM), lambda i: (0, r0 + i, 0)),
                 pl.BlockSpec((nkv, tm, HEAD_DIM), lambda i: (0, r0 + i, 0)),
                 pl.BlockSpec((nkv, tm, HEAD_DIM), lambda i: (0, r0 + i, 0))]
    out_shape = [jax.ShapeDtypeStruct((nq, t_total, HEAD_DIM), BF16),
                 jax.ShapeDtypeStruct((nkv, t_total, HEAD_DIM), BF16),
                 jax.ShapeDtypeStruct((nkv, t_total, HEAD_DIM), BF16)]
    if emit_f32:
        out_specs += [pl.BlockSpec((tm, kw), lambda i: (i, 0)),
                      pl.BlockSpec((tm, kw), lambda i: (i, 0))]
        out_shape += [jax.ShapeDtypeStruct((n_rows, kw), F32),
                      jax.ShapeDtypeStruct((n_rows, kw), F32)]
    aliases = {}
    if prev is not None:
        n_in = len(args)
        in_specs += [pl.BlockSpec(memory_space=pl.ANY)] * 3
        args += list(prev)
        aliases = {n_in: 0, n_in + 1: 1, n_in + 2: 2}

    def kern(*refs):
        if prev is not None:
            n_in = len(args) - 3
            refs = refs[:n_in] + refs[n_in + 3:]
        _prep_kernel(*refs, nq=nq, nkv=nkv, rope=rope, emit_f32=emit_f32)

    return pl.pallas_call(
        kern,
        grid=(n_rows // tm,),
        in_specs=in_specs,
        out_specs=out_specs,
        out_shape=out_shape,
        input_output_aliases=aliases,
        compiler_params=_cparams(("arbitrary",), 40),
        name="qkv_prep_rope" if rope else "qkv_prep",
    )(*args)


def _softmax_pv(s, v_list, widths, sink):
    m = jnp.max(s, axis=-1, keepdims=True)
    if sink is not None:
        m = jnp.maximum(m, sink)
    p = jnp.exp(s - m)
    l = jnp.sum(p, axis=-1, keepdims=True)
    if sink is not None:
        l = l + jnp.exp(sink - m)
    pb = p.astype(BF16)
    o = None
    c0 = 0
    for v, w in zip(v_list, widths):
        part = _dot(pb[:, c0:c0 + w], v)
        o = part if o is None else o + part
        c0 += w
    return o / l


def _ctx_attn_kernel(*refs, has_sink):
    if has_sink:
        q_ref, k_ref, v_ref, sink_ref, o_ref = refs
    else:
        q_ref, k_ref, v_ref, o_ref = refs
    k = k_ref[0]
    v = v_ref[0]
    for g in range(GQA_GROUP):
        s = _dot_t(q_ref[g], k)
        sink = sink_ref[g][:, 0:1] if has_sink else None
        o = _softmax_pv(s, [v], [k.shape[0]], sink)
        o_ref[:, g * HEAD_DIM:(g + 1) * HEAD_DIM] = o.astype(BF16)


def _ctx_attention(rows, q, k, v, sink, att_cols):
    nkv = k.shape[0]
    s = rows.s_ctx
    gw = GQA_GROUP * HEAD_DIM
    has_sink = sink is not None
    in_specs = [pl.BlockSpec((GQA_GROUP, s, HEAD_DIM), lambda b, h: (h, b, 0)),
                pl.BlockSpec((1, s, HEAD_DIM), lambda b, h: (h, b, 0)),
                pl.BlockSpec((1, s, HEAD_DIM), lambda b, h: (h, b, 0))]
    args = [q, k, v]
    if has_sink:
        in_specs.append(pl.BlockSpec((GQA_GROUP, 1, HEAD_DIM), lambda b, h: (h, 0, 0)))
        args.append(sink)
    return pl.pallas_call(
        functools.partial(_ctx_attn_kernel, has_sink=has_sink),
        grid=(rows.n_ctx_seq, nkv),
        in_specs=in_specs,
        out_specs=pl.BlockSpec((s, gw), lambda b, h: (b, h)),
        out_shape=jax.ShapeDtypeStruct((rows.t, att_cols), BF16),
        compiler_params=_cparams(("arbitrary", "arbitrary"), 40),
        name="ctx_attention",
    )(*args)


def _window_attn_kernel(q_ref, kp_ref, kc_ref, kn_ref, vp_ref, vc_ref, vn_ref, kx_ref, vx_ref,
                        sink_ref, prev_ref, o_ref, *, nkv, n_blocks):
    del prev_ref
    n = pl.program_id(1)
    qb = WINDOW
    m_rows = GQA_GROUP * qb
    r = lax.broadcasted_iota(jnp.int32, (m_rows, qb), 0) & (qb - 1)
    c = lax.broadcasted_iota(jnp.int32, (m_rows, qb), 1)
    ok_prev = (c >= r) & (n > 0)
    ok_next = (c <= r) & (n < n_blocks - 1)
    n_ctx = kx_ref.shape[2]
    for h in range(nkv):
        q = q_ref[h * GQA_GROUP:(h + 1) * GQA_GROUP].reshape(m_rows, HEAD_DIM)
        s = jnp.concatenate(
            [jnp.where(ok_prev, _dot_t(q, kp_ref[h]), NEG_INF),
             _dot_t(q, kc_ref[h]),
             jnp.where(ok_next, _dot_t(q, kn_ref[h]), NEG_INF),
             _dot_t(q, kx_ref[0, h])], axis=-1)
        sink = jnp.concatenate(
            [jnp.broadcast_to(sink_ref[h * GQA_GROUP + g][:, 0:1], (qb, 1)) for g in range(GQA_GROUP)],
            axis=0)
        o = _softmax_pv(s, [vp_ref[h], vc_ref[h], vn_ref[h], vx_ref[0, h]], [qb, qb, qb, n_ctx], sink)
        for g in range(GQA_GROUP):
            col = (h * GQA_GROUP + g) * HEAD_DIM
            o_ref[:, col:col + HEAD_DIM] = o[g * qb:(g + 1) * qb].astype(BF16)


def _window_attention(rows, q, k, v, kx, vx, sink, prev):
    nq, nkv = q.shape[0], k.shape[0]
    qb = WINDOW
    nb = rows.s_lat // qb
    base = rows.tc // qb
    n_ctx = kx.shape[2]

    def blk(b, n):
        return base + b * nb + n

    def kv_spec(shift):
        def imap(b, n):
            return (0, base + b * nb + jnp.clip(n + shift, 0, nb - 1), 0)
        return pl.BlockSpec((nkv, qb, HEAD_DIM), imap)

    cache_spec = pl.BlockSpec((1, nkv, n_ctx, HEAD_DIM), lambda b, n: (b, 0, 0, 0))
    return pl.pallas_call(
        functools.partial(_window_attn_kernel, nkv=nkv, n_blocks=nb),
        grid=(rows.n_lat_seq, nb),
        in_specs=[pl.BlockSpec((nq, qb, HEAD_DIM), lambda b, n: (0, blk(b, n), 0)),
                  kv_spec(-1), kv_spec(0), kv_spec(1),
                  kv_spec(-1), kv_spec(0), kv_spec(1),
                  cache_spec, cache_spec,
                  pl.BlockSpec((nq, 1, HEAD_DIM), lambda b, n: (0, 0, 0)),
                  pl.BlockSpec(memory_space=pl.ANY)],
        out_specs=pl.BlockSpec((qb, nq * HEAD_DIM), lambda b, n: (blk(b, n), 0)),
        out_shape=jax.ShapeDtypeStruct(prev.shape, BF16),
        input_output_aliases={10: 0},
        compiler_params=_cparams(("arbitrary", "arbitrary"), 40),
        name="window_attention",
    )(q, k, k, k, v, v, v, kx, vx, sink, prev)


def _dense_attn_kernel(q_ref, k_ref, v_ref, kx_ref, vx_ref, prev_ref, o_ref, m_ref, l_ref, acc_ref,
                       *, bq, bk):
    del prev_ref
    m_rows = GQA_GROUP * bq
    q = q_ref[...].reshape(m_rows, HEAD_DIM)
    m_ref[...] = jnp.full((m_rows, 1), NEG_INF, F32)
    l_ref[...] = jnp.zeros((m_rows, 1), F32)
    acc_ref[...] = jnp.zeros((m_rows, HEAD_DIM), F32)

    def update(k, v):
        s = _dot_t(q, k)
        m_old = m_ref[...]
        m_new = jnp.maximum(m_old, jnp.max(s, axis=-1, keepdims=True))
        alpha = jnp.exp(m_old - m_new)
        p = jnp.exp(s - m_new)
        l_ref[...] = alpha * l_ref[...] + jnp.sum(p, axis=-1, keepdims=True)
        acc_ref[...] = alpha * acc_ref[...] + _dot(p.astype(BF16), v)
        m_ref[...] = m_new

    def body(c, carry):
        off = pl.multiple_of(c * bk, bk)
        update(k_ref[0, pl.ds(off, bk), :], v_ref[0, pl.ds(off, bk), :])
        return carry

    lax.fori_loop(0, k_ref.shape[1] // bk, body, 0)
    update(kx_ref[0, 0], vx_ref[0, 0])
    o = acc_ref[...] / l_ref[...]
    for g in range(GQA_GROUP):
        o_ref[:, g * HEAD_DIM:(g + 1) * HEAD_DIM] = o[g * bq:(g + 1) * bq].astype(BF16)


def _dense_attention(rows, q, k, v, kx, vx, prev):
    nkv = k.shape[0]
    bq, bk = 256, 512
    s = rows.s_lat
    assert rows.tc % s == 0 and s % bk == 0
    seq0 = rows.tc // s
    nqb = s // bq
    base = rows.tc // bq
    n_ctx = kx.shape[2]
    m_rows = GQA_GROUP * bq
    cache_spec = pl.BlockSpec((1, 1, n_ctx, HEAD_DIM), lambda b, h, i: (b, h, 0, 0))
    return pl.pallas_call(
        functools.partial(_dense_attn_kernel, bq=bq, bk=bk),
        grid=(rows.n_lat_seq, nkv, nqb),
        in_specs=[pl.BlockSpec((GQA_GROUP, bq, HEAD_DIM), lambda b, h, i: (h, base + b * nqb + i, 0)),
                  pl.BlockSpec((1, s, HEAD_DIM), lambda b, h, i: (h, seq0 + b, 0)),
                  pl.BlockSpec((1, s, HEAD_DIM), lambda b, h, i: (h, seq0 + b, 0)),
                  cache_spec, cache_spec,
                  pl.BlockSpec(memory_space=pl.ANY)],
        out_specs=pl.BlockSpec((bq, GQA_GROUP * HEAD_DIM), lambda b, h, i: (base + b * nqb + i, h)),
        out_shape=jax.ShapeDtypeStruct(prev.shape, BF16),
        scratch_shapes=[pltpu.VMEM((m_rows, 1), F32), pltpu.VMEM((m_rows, 1), F32),
                        pltpu.VMEM((m_rows, HEAD_DIM), F32)],
        input_output_aliases={5: 0},
        compiler_params=_cparams(("arbitrary", "arbitrary", "arbitrary"), 40),
        name="dense_attention",
    )(q, k, v, kx, vx, prev)


def _dft_tables(n):
    j = jnp.arange(n, dtype=jnp.int32)
    jk = (j[:, None] * j[None, :]) % n
    ang = jk.astype(F32) * (2.0 * math.pi / n)
    scale = n ** -0.5
    return (jnp.cos(ang) * scale).astype(BF16), (jnp.sin(ang) * scale).astype(BF16)


def _chan_dft_kernel(a_ref, cs_ref, uc_ref, us_ref, *, gd):
    for g in range(A_GROUPS):
        sl = slice(g * gd, (g + 1) * gd)
        y = _dot(a_ref[:, sl].astype(BF16), cs_ref[...])
        uc_ref[:, sl] = y[:, :gd].astype(BF16)
        us_ref[:, sl] = y[:, gd:].astype(BF16)


def _chan_dft(proj, a_width, cs):
    t = proj.shape[0]
    gd = a_width // A_GROUPS
    tm = ROW_TILE
    return pl.pallas_call(
        functools.partial(_chan_dft_kernel, gd=gd),
        grid=(t // tm,),
        in_specs=[pl.BlockSpec((tm, a_width), lambda i: (i, 0)),
                  pl.BlockSpec((gd, 2 * gd), lambda i: (0, 0))],
        out_specs=[pl.BlockSpec((tm, a_width), lambda i: (i, 0)),
                   pl.BlockSpec((tm, a_width), lambda i: (i, 0))],
        out_shape=[jax.ShapeDtypeStruct((t, a_width), BF16)] * 2,
        compiler_params=_cparams(("arbitrary",), 40),
        name="channel_dft",
    )(proj, cs)


def _seq_dft_kernel(c_ref, s_ref, uc_ref, us_ref, *rest):
    o_ref = rest[-1]
    o_ref[...] = (_dot(c_ref[...], uc_ref[...]) - _dot(s_ref[...], us_ref[...])).astype(BF16)


def _seq_dft(uc, us, cmat, smat, row0, n_seq, prev):
    t, width = uc.shape
    s = cmat.shape[0]
    tm = min(s, ROW_TILE)
    tn = 512
    ni = s // tm
    seq0 = row0 // s
    out0 = row0 // tm
    in_specs = [pl.BlockSpec((tm, s), lambda b, j, i: (i, 0)),
                pl.BlockSpec((tm, s), lambda b, j, i: (i, 0)),
                pl.BlockSpec((s, tn), lambda b, j, i: (seq0 + b, j)),
                pl.BlockSpec((s, tn), lambda b, j, i: (seq0 + b, j))]
    args = [cmat, smat, uc, us]
    aliases = {}
    if prev is not None:
        in_specs.append(pl.BlockSpec(memory_space=pl.ANY))
        args.append(prev)
        aliases = {4: 0}
    return pl.pallas_call(
        _seq_dft_kernel,
        grid=(n_seq, width // tn, ni),
        in_specs=in_specs,
        out_specs=pl.BlockSpec((tm, tn), lambda b, j, i: (out0 + b * ni + i, j)),
        out_shape=jax.ShapeDtypeStruct((t, width), BF16),
        input_output_aliases=aliases,
        compiler_params=_cparams(("arbitrary", "arbitrary", "arbitrary"), 48),
        name="sequence_dft",
    )(*args)


def _out_proj_kernel(*refs, n_lhs):
    lhs = refs[:n_lhs]
    ws = refs[n_lhs:2 * n_lhs]
    x_ref, mod_ref, o_ref = refs[2 * n_lhs:]
    acc = None
    for a, w in zip(lhs, ws):
        part = _dot(a[...], w[...])
        acc = part if acc is None else acc + part
    o_ref[...] = x_ref[...] + mod_ref[0, 2:3, :] * acc


def _out_proj(rows, lhs_list, w_bf16, x, mod_l):
    t, d = x.shape
    tm, tn = ROW_TILE, 512
    in_specs, args, w_args, w_specs = [], [], [], []
    k0 = 0
    for a in lhs_list:
        kw = a.shape[1]
        in_specs.append(pl.BlockSpec((tm, kw), lambda i, j: (i, 0)))
        args.append(a)
        kb = k0 // kw
        assert k0 % kw == 0
        w_specs.append(pl.BlockSpec((kw, tn), lambda i, j, kb=kb: (kb, j)))
        w_args.append(w_bf16)
        k0 += kw
    in_specs += w_specs + [pl.BlockSpec((tm, tn), lambda i, j: (i, j)),
                           pl.BlockSpec((1, 6, tn), lambda i, j: (rows.mod_row(i, tm), 0, j))]
    args += w_args + [x, mod_l]
    return pl.pallas_call(
        functools.partial(_out_proj_kernel, n_lhs=len(lhs_list)),
        grid=(t // tm, d // tn),
        in_specs=in_specs,
        out_specs=pl.BlockSpec((tm, tn), lambda i, j: (i, j)),
        out_shape=jax.ShapeDtypeStruct((t, d), F32),
        compiler_params=_cparams(("arbitrary", "arbitrary"), 40),
        name="out_proj",
    )(*args)


def _ffn_norm_kernel(x_ref, mod_ref, g_ref, whi_ref, wlo_ref, b_ref, h_ref, lg_ref):
    h = _norm_mod(x_ref[...], g_ref[...], mod_ref[0, 4:5, :], mod_ref[0, 3:4, :])
    hi = h.astype(BF16)
    lo = (h - hi.astype(F32)).astype(BF16)
    h_ref[...] = hi
    lg_ref[...] = (_dot(hi, whi_ref[...]) + _dot(hi, wlo_ref[...]) + _dot(lo, whi_ref[...])
                   + b_ref[...])


def _ffn_norm_router(rows, x, mod_l, g, w_hi, w_lo, b_r):
    t, d = x.shape
    tm = ROW_TILE
    return pl.pallas_call(
        _ffn_norm_kernel,
        grid=(t // tm,),
        in_specs=[pl.BlockSpec((tm, d), lambda i: (i, 0)),
                  pl.BlockSpec((1, 6, d), lambda i: (rows.mod_row(i, tm), 0, 0)),
                  pl.BlockSpec((1, d), lambda i: (0, 0)),
                  pl.BlockSpec((d, LOGIT_LANES), lambda i: (0, 0)),
                  pl.BlockSpec((d, LOGIT_LANES), lambda i: (0, 0)),
                  pl.BlockSpec((1, LOGIT_LANES), lambda i: (0, 0))],
        out_specs=[pl.BlockSpec((tm, d), lambda i: (i, 0)),
                   pl.BlockSpec((tm, LOGIT_LANES), lambda i: (i, 0))],
        out_shape=[jax.ShapeDtypeStruct((t, d), BF16),
                   jax.ShapeDtypeStruct((t, LOGIT_LANES), F32)],
        compiler_params=_cparams(("arbitrary",), 40),
        name="ffn_norm_router",
    )(x, mod_l, g.reshape(1, d), w_hi, w_lo, b_r)


def _expert_kernel(be_ref, nb_ref, x_ref, wg_ref, wu_ref, wd_ref, rw_ref, o_ref):
    del be_ref
    b = pl.program_id(0)

    @pl.when(b < nb_ref[0])
    def _():
        x = x_ref[...]
        gate = _dot(x, wg_ref[0].astype(BF16))
        up = _dot(x, wu_ref[0].astype(BF16))
        hid = (_silu(gate) * up).astype(BF16)
        o_ref[...] = _dot(hid, wd_ref[0].astype(BF16)) * rw_ref[...]

    @pl.when(b >= nb_ref[0])
    def _():
        o_ref[...] = jnp.zeros(o_ref.shape, F32)


def _experts(buf, row_w, blk_expert, n_used, w_gate, w_up, w_down):
    nrows, d = buf.shape
    de = w_gate.shape[2]
    bm = MOE_ROWS
    nblk = nrows // bm

    def x_map(b, be, nb):
        return (jnp.minimum(b, jnp.maximum(nb[0] - 1, 0)), 0)

    grid_spec = pltpu.PrefetchScalarGridSpec(
        num_scalar_prefetch=2,
        grid=(nblk,),
        in_specs=[pl.BlockSpec((bm, d), x_map),
                  pl.BlockSpec((1, d, de), lambda b, be, nb: (be[b], 0, 0)),
                  pl.BlockSpec((1, d, de), lambda b, be, nb: (be[b], 0, 0)),
                  pl.BlockSpec((1, de, d), lambda b, be, nb: (be[b], 0, 0)),
                  pl.BlockSpec((bm, 1), x_map)],
        out_specs=pl.BlockSpec((bm, d), lambda b, be, nb: (b, 0)),
    )
    return pl.pallas_call(
        _expert_kernel,
        grid_spec=grid_spec,
        out_shape=jax.ShapeDtypeStruct((nrows, d), F32),
        compiler_params=_cparams(("arbitrary",), 56),
        name="moe_experts",
    )(blk_expert, n_used, buf, w_gate, w_up, w_down, row_w)


def _combine_kernel(x_ref, y_ref, mod_ref, o_ref):
    d = x_ref.shape[1]
    o_ref[...] = x_ref[...] + mod_ref[0, 5:6, :] * (y_ref[:, :d] + y_ref[:, d:])


def _combine(rows, x, y_pairs, mod_l):
    t, d = x.shape
    tm = 256
    return pl.pallas_call(
        _combine_kernel,
        grid=(t // tm,),
        in_specs=[pl.BlockSpec((tm, d), lambda i: (i, 0)),
                  pl.BlockSpec((tm, 2 * d), lambda i: (i, 0)),
                  pl.BlockSpec((1, 6, d), lambda i: (rows.mod_row(i, tm), 0, 0))],
        out_specs=pl.BlockSpec((tm, d), lambda i: (i, 0)),
        out_shape=jax.ShapeDtypeStruct((t, d), F32),
        compiler_params=_cparams(("arbitrary",), 40),
        name="moe_combine",
    )(x, y_pairs, mod_l)


def _moe(rows, x, mod_l, g, w_group, b_group, w_route, b_route, w_gate, w_up, w_down):
    t, d = x.shape
    n_exp = w_route.shape[1]
    per_group = n_exp // N_GROUPS
    pad = LOGIT_LANES - N_GROUPS - n_exp
    w_r = jnp.concatenate([w_group, w_route, jnp.zeros((d, pad), F32)], axis=1)
    b_r = jnp.concatenate([b_group, b_route, jnp.zeros((pad,), F32)]).reshape(1, LOGIT_LANES)
    w_hi = w_r.astype(BF16)
    w_lo = (w_r - w_hi.astype(F32)).astype(BF16)
    h, logits = _ffn_norm_router(rows, x, mod_l, g, w_hi, w_lo, b_r)

    g_logit = logits[:, :N_GROUPS]
    g_idx = jnp.argmax(g_logit, axis=-1).astype(jnp.int32)
    g_w = 1.0 / jnp.sum(jnp.exp(g_logit - jnp.max(g_logit, axis=-1, keepdims=True)), axis=-1, keepdims=True)
    e_all = logits[:, N_GROUPS:N_GROUPS + n_exp].reshape(t, N_GROUPS, per_group)
    e_logit = jnp.take_along_axis(e_all, g_idx[:, None, None], axis=1)[:, 0]
    top_p, top_i = lax.top_k(jax.nn.softmax(e_logit, axis=-1), TOP_K)
    wts = (g_w * top_p / jnp.sum(top_p, axis=-1, keepdims=True)).reshape(-1)
    eid = (g_idx[:, None] * per_group + top_i).reshape(-1).astype(jnp.int32)

    bm = MOE_ROWS
    n_assign = t * TOP_K
    order = jnp.argsort(eid)
    e_sorted = eid[order]
    sizes = jnp.bincount(eid, length=n_exp).astype(jnp.int32)
    padded = ((sizes + bm - 1) // bm) * bm
    start = jnp.cumsum(sizes) - sizes
    pad_end = jnp.cumsum(padded)
    pad_start = pad_end - padded
    dest = pad_start[e_sorted] + (jnp.arange(n_assign, dtype=jnp.int32) - start[e_sorted])
    nblk = -(-(n_assign + n_exp * (bm - 1)) // bm)
    blk_start = jnp.arange(nblk, dtype=jnp.int32) * bm
    blk_expert = jnp.minimum(jnp.searchsorted(pad_end, blk_start, side='right'), n_exp - 1).astype(jnp.int32)
    n_used = (pad_end[-1:] // bm).astype(jnp.int32)
    buf = jnp.zeros((nblk * bm, d), BF16).at[dest].set(h[order // TOP_K])
    row_w = jnp.zeros((nblk * bm, 1), F32).at[dest, 0].set(wts[order])
    ys = _experts(buf, row_w, blk_expert, n_used, w_gate, w_up, w_down)
    slot = jnp.zeros((n_assign,), jnp.int32).at[order].set(dest)
    y_pairs = ys[slot].reshape(t, TOP_K * d)
    return _combine(rows, x, y_pairs, mod_l)


def _rope_tables(s_lat):
    n_rows = s_lat // GRID_W
    row = jnp.repeat(jnp.arange(n_rows, dtype=F32), GRID_W)
    col = jnp.tile(jnp.arange(GRID_W, dtype=F32), n_rows)
    axis_dim = HEAD_DIM // 2
    inv_freq = ROPE_THETA ** (-jnp.arange(0, axis_dim, 2, dtype=F32) / axis_dim)
    ar, ac = row[:, None] * inv_freq, col[:, None] * inv_freq
    cos = jnp.concatenate([jnp.cos(ar), jnp.cos(ar), jnp.cos(ac), jnp.cos(ac)], axis=1)
    sin = jnp.concatenate([-jnp.sin(ar), jnp.sin(ar), -jnp.sin(ac), jnp.sin(ac)], axis=1)
    return cos, sin


def _cache_heads(c):
    return jnp.transpose(c, (0, 2, 1, 3)).astype(BF16)


def kernel(x_prompt, x_sample, cache_b_k, cache_b_v, cache_c_k, cache_c_v, c, c_ctx,
           w_mod, b_mod, norm_mix, norm_ffn, w_in_even, w_out_even, q_norm_b, k_norm_b, sink_b,
           w_in_odd, w_out_odd, q_norm_c, k_norm_c, w_group, b_group, w_route, b_route,
           w_gate, w_up, w_down):
    n_ctx_seq, s_ctx, d = x_prompt.shape
    n_lat_seq, s_lat, _ = x_sample.shape
    depth = w_mod.shape[0]
    rows = _Rows(n_ctx_seq, s_ctx, n_lat_seq, s_lat)
    assert 1 + n_lat_seq <= MOD_ROWS and rows.tc % ROW_TILE == 0 and s_lat % ROW_TILE == 0
    a_width = A_GROUPS * (d // 8)
    b_heads = (d - a_width) // HEAD_DIM
    b_kv = b_heads // GQA_GROUP
    c_heads = d // HEAD_DIM
    c_kv = c_heads // GQA_GROUP

    cvec = jnp.concatenate([c_ctx[None, :], c, jnp.zeros((MOD_ROWS - 1 - n_lat_seq, d), F32)], axis=0)
    mod = _modulation(cvec, w_mod, b_mod).reshape(depth, MOD_ROWS, 6, d)

    x = jnp.concatenate([x_prompt.reshape(rows.tc, d), x_sample.reshape(rows.tl, d)], axis=0)
    rope = _rope_tables(s_lat)
    gd = a_width // A_GROUPS
    cc, sc = _dft_tables(gd)
    chan_cs = jnp.concatenate([cc, sc], axis=1)
    dft_ctx = _dft_tables(s_ctx)
    dft_lat = _dft_tables(s_lat)

    new_kv = []
    for l in range(depth):
        i = l // 2
        mod_l = mod[l]
        if l % 2 == 0:
            proj = _in_proj(rows, x, mod_l, norm_mix[l], w_in_even[i].astype(BF16))
            qc, kc, vc, kf, vf = _prep(proj, a_width, b_heads, b_kv, q_norm_b[i], k_norm_b[i],
                                       0, rows.tc, None, True, rows.t, None)
            q, k, v = _prep(proj, a_width, b_heads, b_kv, q_norm_b[i], k_norm_b[i],
                            rows.tc, rows.tl, rope, False, rows.t, (qc, kc, vc))
            sink = jnp.broadcast_to(sink_b[i][:, None, None], (b_heads, 1, HEAD_DIM))
            att = _ctx_attention(rows, q, k, v, sink, b_heads * HEAD_DIM)
            att = _window_attention(rows, q, k, v, _cache_heads(cache_b_k[:, i]),
                                    _cache_heads(cache_b_v[:, i]), sink, att)
            uc, us = _chan_dft(proj, a_width, chan_cs)
            fm = _seq_dft(uc, us, dft_ctx[0], dft_ctx[1], 0, n_ctx_seq, None)
            fm = _seq_dft(uc, us, dft_lat[0], dft_lat[1], rows.tc, n_lat_seq, fm)
            x = _out_proj(rows, [fm, att], w_out_even[i].astype(BF16), x, mod_l)
            nkv = b_kv
        else:
            proj = _in_proj(rows, x, mod_l, norm_mix[l], w_in_odd[i].astype(BF16))
            qc, kc, vc, kf, vf = _prep(proj, 0, c_heads, c_kv, q_norm_c[i], k_norm_c[i],
                                       0, rows.tc, None, True, rows.t, None)
            q, k, v = _prep(proj, 0, c_heads, c_kv, q_norm_c[i], k_norm_c[i],
                            rows.tc, rows.tl, rope, False, rows.t, (qc, kc, vc))
            att = _ctx_attention(rows, q, k, v, None, c_heads * HEAD_DIM)
            att = _dense_attention(rows, q, k, v, _cache_heads(cache_c_k[:, i]),
                                   _cache_heads(cache_c_v[:, i]), att)
            x = _out_proj(rows, [att], w_out_odd[i].astype(BF16), x, mod_l)
            nkv = c_kv
        new_kv.append((kf.reshape(n_ctx_seq, s_ctx, nkv, HEAD_DIM),
                       vf.reshape(n_ctx_seq, s_ctx, nkv, HEAD_DIM)))
        x = _moe(rows, x, mod_l, norm_ffn[l], w_group[l], b_group[l], w_route[l], b_route[l],
                 w_gate[l], w_up[l], w_down[l])

    y_prompt = x[:rows.tc].reshape(n_ctx_seq, s_ctx, d)
    y_sample = x[rows.tc:].reshape(n_lat_seq, s_lat, d)
    new_b_k = jnp.stack([new_kv[l][0] for l in range(0, depth, 2)], axis=1)
    new_b_v = jnp.stack([new_kv[l][1] for l in range(0, depth, 2)], axis=1)
    new_c_k = jnp.stack([new_kv[l][0] for l in range(1, depth, 2)], axis=1)
    new_c_v = jnp.stack([new_kv[l][1] for l in range(1, depth, 2)], axis=1)
    return (y_prompt, y_sample, new_b_k, new_b_v, new_c_k, new_c_v)
```

```python
import functools
import math

import jax
import jax.numpy as jnp
from jax import lax
from jax.experimental import pallas as pl
from jax.experimental.pallas import tpu as pltpu

F32 = jnp.float32
BF16 = jnp.bfloat16

HEAD_DIM = 128
GRID_W = 64
ROPE_THETA = 10000.0
WINDOW = 128
RMS_EPS = 1e-6
NEG_INF = -1e30
A_GROUPS = 4
N_GROUPS = 8
TOP_K = 2
GQA_GROUP = 4
MOD_ROWS = 8
LOGIT_LANES = 128
ROW_TILE = 512
MOE_ROWS = 256
LOG2E = math.log2(math.e)
MIB = 1024 * 1024


def _cparams(sem, vmem_mib):
    return pltpu.CompilerParams(dimension_semantics=sem, vmem_limit_bytes=vmem_mib * MIB)


def _silu(x):
    return x / (1.0 + jnp.exp(-x))


def _dot(a, b):
    return jnp.dot(a, b, preferred_element_type=F32)


def _dot_t(a, b):
    return lax.dot_general(a, b, (((1,), (1,)), ((), ())), preferred_element_type=F32)


def _mod_kernel(c_ref, w_ref, b_ref, o_ref):
    s = _silu(c_ref[...]).astype(BF16)
    o_ref[0] = _dot(s, w_ref[0].astype(BF16)) + b_ref[0]


def _modulation(cvec, w_mod, b_mod):
    depth, d, n6 = w_mod.shape
    tn = 1024
    return pl.pallas_call(
        _mod_kernel,
        grid=(depth, n6 // tn),
        in_specs=[pl.BlockSpec((MOD_ROWS, d), lambda l, j: (0, 0)),
                  pl.BlockSpec((1, d, tn), lambda l, j: (l, 0, j)),
                  pl.BlockSpec((1, 1, tn), lambda l, j: (l, 0, j))],
        out_specs=pl.BlockSpec((1, MOD_ROWS, tn), lambda l, j: (l, 0, j)),
        out_shape=jax.ShapeDtypeStruct((depth, MOD_ROWS, n6), F32),
        compiler_params=_cparams(("arbitrary", "arbitrary"), 40),
        name="modulation",
    )(cvec, w_mod, b_mod.reshape(depth, 1, n6))


class _Rows:
    def __init__(self, n_ctx_seq, s_ctx, n_lat_seq, s_lat):
        self.n_ctx_seq, self.s_ctx, self.n_lat_seq, self.s_lat = n_ctx_seq, s_ctx, n_lat_seq, s_lat
        self.tc = n_ctx_seq * s_ctx
        self.tl = n_lat_seq * s_lat
        self.t = self.tc + self.tl

    def mod_row(self, i, tm):
        nct = self.tc // tm
        per = self.s_lat // tm
        return jnp.where(i < nct, 0, 1 + (i - nct) // per)


def _norm_mod(x, g, scale, shift):
    ms = jnp.mean(x * x, axis=-1, keepdims=True)
    y = x * lax.rsqrt(ms + RMS_EPS) * g
    return y * (1.0 + scale) + shift


def _in_proj_kernel(x_ref, mod_ref, g_ref, w_ref, o_ref, h_ref):
    @pl.when(pl.program_id(1) == 0)
    def _():
        h = _norm_mod(x_ref[...], g_ref[...], mod_ref[0, 1:2, :], mod_ref[0, 0:1, :])
        h_ref[...] = h.astype(BF16)

    o_ref[...] = _dot(h_ref[...], w_ref[...])


def _in_proj(rows, x, mod_l, g, w_bf16):
    t, d = x.shape
    n = w_bf16.shape[1]
    tm, tn = ROW_TILE, 512
    return pl.pallas_call(
        _in_proj_kernel,
        grid=(t // tm, n // tn),
        in_specs=[pl.BlockSpec((tm, d), lambda i, j: (i, 0)),
                  pl.BlockSpec((1, 6, d), lambda i, j: (rows.mod_row(i, tm), 0, 0)),
                  pl.BlockSpec((1, d), lambda i, j: (0, 0)),
                  pl.BlockSpec((d, tn), lambda i, j: (0, j))],
        out_specs=pl.BlockSpec((tm, tn), lambda i, j: (i, j)),
        out_shape=jax.ShapeDtypeStruct((t, n), F32),
        scratch_shapes=[pltpu.VMEM((tm, d), BF16)],
        compiler_params=_cparams(("arbitrary", "arbitrary"), 40),
        name="in_proj",
    )(x, mod_l, g.reshape(1, d), w_bf16)


def _swap32(y):
    lane = lax.broadcasted_iota(jnp.int32, y.shape, 1)
    return jnp.where((lane & 63) < 32, pltpu.roll(y, 96, 1), pltpu.roll(y, 32, 1))


def _prep_kernel(*refs, nq, nkv, rope, emit_f32, transposed, q_scale):
    q_ref, k_ref, v_ref, qn_ref, kn_ref = refs[:5]
    pos = 5
    if rope:
        cos_ref, sin_ref = refs[pos:pos + 2]
        pos += 2
    qo_ref, ko_ref, vo_ref = refs[pos:pos + 3]
    pos += 3
    if emit_f32:
        kf_ref, vf_ref = refs[pos:pos + 2]

    def head_norm(x, gain):
        ms = jnp.mean(x * x, axis=-1, keepdims=True)
        return x * lax.rsqrt(ms + RMS_EPS) * gain

    def rot(y):
        if not rope:
            return y
        return y * cos_ref[...] + _swap32(y) * sin_ref[...]

    for h in range(nq):
        sl = slice(h * HEAD_DIM, (h + 1) * HEAD_DIM)
        y = rot(head_norm(q_ref[:, sl], qn_ref[...])) * q_scale
        qo_ref[h] = (y.T if transposed else y).astype(BF16)
    for h in range(nkv):
        sl = slice(h * HEAD_DIM, (h + 1) * HEAD_DIM)
        y = head_norm(k_ref[:, sl], kn_ref[...])
        v = v_ref[:, sl]
        if emit_f32:
            kf_ref[:, sl] = y
            vf_ref[:, sl] = v
        ko_ref[h] = rot(y).astype(BF16)
        vo_ref[h] = (v.T if transposed else v).astype(BF16)


def _prep(proj, col0, nq, nkv, qn, kn, row0, n_rows, rope_tabs, emit_f32, transposed, q_scale):
    tm = 256
    rope = rope_tabs is not None
    r0 = row0 // tm
    qw, kw = nq * HEAD_DIM, nkv * HEAD_DIM
    qb, kb, vb = col0 // qw, (col0 + qw) // kw, (col0 + qw + kw) // kw
    assert col0 % qw == 0 and (col0 + qw) % kw == 0 and row0 % tm == 0
    in_specs = [pl.BlockSpec((tm, qw), lambda i: (r0 + i, qb)),
                pl.BlockSpec((tm, kw), lambda i: (r0 + i, kb)),
                pl.BlockSpec((tm, kw), lambda i: (r0 + i, vb)),
                pl.BlockSpec((1, HEAD_DIM), lambda i: (0, 0)),
                pl.BlockSpec((1, HEAD_DIM), lambda i: (0, 0))]
    args = [proj, proj, proj, qn.reshape(1, HEAD_DIM), kn.reshape(1, HEAD_DIM)]
    if rope:
        per = rope_tabs[0].shape[0] // tm
        in_specs += [pl.BlockSpec((tm, HEAD_DIM), lambda i: (i % per, 0)),
                     pl.BlockSpec((tm, HEAD_DIM), lambda i: (i % per, 0))]
        args += list(rope_tabs)
    row_major = pl.BlockSpec((nq, tm, HEAD_DIM), lambda i: (0, i, 0))
    col_major = pl.BlockSpec((nq, HEAD_DIM, tm), lambda i: (0, 0, i))
    kv_row = pl.BlockSpec((nkv, tm, HEAD_DIM), lambda i: (0, i, 0))
    kv_col = pl.BlockSpec((nkv, HEAD_DIM, tm), lambda i: (0, 0, i))
    if transposed:
        out_specs = [col_major, kv_row, kv_col]
        out_shape = [jax.ShapeDtypeStruct((nq, HEAD_DIM, n_rows), BF16),
                     jax.ShapeDtypeStruct((nkv, n_rows, HEAD_DIM), BF16),
                     jax.ShapeDtypeStruct((nkv, HEAD_DIM, n_rows), BF16)]
    else:
        out_specs = [row_major, kv_row, kv_row]
        out_shape = [jax.ShapeDtypeStruct((nq, n_rows, HEAD_DIM), BF16),
                     jax.ShapeDtypeStruct((nkv, n_rows, HEAD_DIM), BF16),
                     jax.ShapeDtypeStruct((nkv, n_rows, HEAD_DIM), BF16)]
    if emit_f32:
        out_specs += [pl.BlockSpec((tm, kw), lambda i: (i, 0)),
                      pl.BlockSpec((tm, kw), lambda i: (i, 0))]
        out_shape += [jax.ShapeDtypeStruct((n_rows, kw), F32),
                      jax.ShapeDtypeStruct((n_rows, kw), F32)]
    return pl.pallas_call(
        functools.partial(_prep_kernel, nq=nq, nkv=nkv, rope=rope, emit_f32=emit_f32,
                          transposed=transposed, q_scale=q_scale),
        grid=(n_rows // tm,),
        in_specs=in_specs,
        out_specs=out_specs,
        out_shape=out_shape,
        compiler_params=_cparams(("arbitrary",), 40),
        name="qkv_prep_rope" if rope else "qkv_prep",
    )(*args)


def _softmax_pv(s, v_list, widths, sink):
    m = jnp.max(s, axis=-1, keepdims=True)
    if sink is not None:
        m = jnp.maximum(m, sink)
    p = jnp.exp(s - m)
    l = jnp.sum(p, axis=-1, keepdims=True)
    if sink is not None:
        l = l + jnp.exp(sink - m)
    pb = p.astype(BF16)
    o = None
    c0 = 0
    for v, w in zip(v_list, widths):
        part = _dot(pb[:, c0:c0 + w], v)
        o = part if o is None else o + part
        c0 += w
    return o / l


def _ctx_attn_kernel(*refs, has_sink):
    if has_sink:
        q_ref, k_ref, v_ref, sink_ref, o_ref = refs
    else:
        q_ref, k_ref, v_ref, o_ref = refs
    k = k_ref[0]
    v = v_ref[0]
    for g in range(GQA_GROUP):
        s = _dot_t(q_ref[g], k)
        sink = sink_ref[g][:, 0:1] if has_sink else None
        o = _softmax_pv(s, [v], [k.shape[0]], sink)
        o_ref[:, g * HEAD_DIM:(g + 1) * HEAD_DIM] = o.astype(BF16)


def _ctx_attention(rows, q, k, v, sink, att_cols):
    nkv = k.shape[0]
    s = rows.s_ctx
    gw = GQA_GROUP * HEAD_DIM
    has_sink = sink is not None
    in_specs = [pl.BlockSpec((GQA_GROUP, s, HEAD_DIM), lambda b, h: (h, b, 0)),
                pl.BlockSpec((1, s, HEAD_DIM), lambda b, h: (h, b, 0)),
                pl.BlockSpec((1, s, HEAD_DIM), lambda b, h: (h, b, 0))]
    args = [q, k, v]
    if has_sink:
        in_specs.append(pl.BlockSpec((GQA_GROUP, 1, HEAD_DIM), lambda b, h: (h, 0, 0)))
        args.append(sink)
    return pl.pallas_call(
        functools.partial(_ctx_attn_kernel, has_sink=has_sink),
        grid=(rows.n_ctx_seq, nkv),
        in_specs=in_specs,
        out_specs=pl.BlockSpec((s, gw), lambda b, h: (b, h)),
        out_shape=jax.ShapeDtypeStruct((rows.t, att_cols), BF16),
        compiler_params=_cparams(("arbitrary", "arbitrary"), 40),
        name="ctx_attention",
    )(*args)


def _window_attn_kernel(q_ref, kp_ref, kc_ref, kn_ref, vp_ref, vc_ref, vn_ref, kx_ref, vx_ref,
                        sink_ref, prev_ref, o_ref, *, nkv, n_blocks):
    del prev_ref
    n = pl.program_id(1)
    qb = WINDOW
    m_rows = GQA_GROUP * qb
    r = lax.broadcasted_iota(jnp.int32, (m_rows, qb), 0) & (qb - 1)
    c = lax.broadcasted_iota(jnp.int32, (m_rows, qb), 1)
    ok_prev = (c >= r) & (n > 0)
    ok_next = (c <= r) & (n < n_blocks - 1)
    n_ctx = kx_ref.shape[2]
    for h in range(nkv):
        q = q_ref[h * GQA_GROUP:(h + 1) * GQA_GROUP].reshape(m_rows, HEAD_DIM)
        s = jnp.concatenate(
            [jnp.where(ok_prev, _dot_t(q, kp_ref[h]), NEG_INF),
             _dot_t(q, kc_ref[h]),
             jnp.where(ok_next, _dot_t(q, kn_ref[h]), NEG_INF),
             _dot_t(q, kx_ref[0, h])], axis=-1)
        sink = jnp.concatenate(
            [jnp.broadcast_to(sink_ref[h * GQA_GROUP + g][:, 0:1], (qb, 1)) for g in range(GQA_GROUP)],
            axis=0)
        o = _softmax_pv(s, [vp_ref[h], vc_ref[h], vn_ref[h], vx_ref[0, h]], [qb, qb, qb, n_ctx], sink)
        for g in range(GQA_GROUP):
            col = (h * GQA_GROUP + g) * HEAD_DIM
            o_ref[:, col:col + HEAD_DIM] = o[g * qb:(g + 1) * qb].astype(BF16)


def _window_attention(rows, q, k, v, kx, vx, sink, prev):
    nq, nkv = q.shape[0], k.shape[0]
    qb = WINDOW
    nb = rows.s_lat // qb
    out0 = rows.tc // qb
    n_ctx = kx.shape[2]

    def kv_spec(shift):
        def imap(b, n):
            return (0, b * nb + jnp.clip(n + shift, 0, nb - 1), 0)
        return pl.BlockSpec((nkv, qb, HEAD_DIM), imap)

    cache_spec = pl.BlockSpec((1, nkv, n_ctx, HEAD_DIM), lambda b, n: (b, 0, 0, 0))
    return pl.pallas_call(
        functools.partial(_window_attn_kernel, nkv=nkv, n_blocks=nb),
        grid=(rows.n_lat_seq, nb),
        in_specs=[pl.BlockSpec((nq, qb, HEAD_DIM), lambda b, n: (0, b * nb + n, 0)),
                  kv_spec(-1), kv_spec(0), kv_spec(1),
                  kv_spec(-1), kv_spec(0), kv_spec(1),
                  cache_spec, cache_spec,
                  pl.BlockSpec((nq, 1, HEAD_DIM), lambda b, n: (0, 0, 0)),
                  pl.BlockSpec(memory_space=pl.ANY)],
        out_specs=pl.BlockSpec((qb, nq * HEAD_DIM), lambda b, n: (out0 + b * nb + n, 0)),
        out_shape=jax.ShapeDtypeStruct(prev.shape, BF16),
        input_output_aliases={10: 0},
        compiler_params=_cparams(("arbitrary", "arbitrary"), 40),
        name="window_attention",
    )(q, k, k, k, v, v, v, kx, vx, sink, prev)


def _dense_attn_kernel(qt_ref, k_ref, vt_ref, kx_ref, vxt_ref, prev_ref, o_ref, *, bq, bk):
    del prev_ref
    m_cols = GQA_GROUP * bq
    qt = jnp.concatenate([qt_ref[g] for g in range(GQA_GROUP)], axis=1)

    def update(k, vt, carry):
        m_old, l_old, acc = carry
        s = _dot(k, qt)
        m_new = jnp.maximum(m_old, jnp.max(s, axis=0, keepdims=True))
        alpha = jnp.exp2(m_old - m_new)
        p = jnp.exp2(s - m_new)
        l_new = alpha * l_old + jnp.sum(p, axis=0, keepdims=True)
        acc = alpha * acc + _dot(vt, p.astype(BF16))
        return m_new, l_new, acc

    carry = (jnp.full((1, m_cols), NEG_INF, F32), jnp.zeros((1, m_cols), F32),
             jnp.zeros((HEAD_DIM, m_cols), F32))
    for c in range(k_ref.shape[1] // bk):
        carry = update(k_ref[0, c * bk:(c + 1) * bk, :], vt_ref[0, :, c * bk:(c + 1) * bk], carry)
    _, l, acc = update(kx_ref[0, 0], vxt_ref[0, 0], carry)
    o = acc / l
    for g in range(GQA_GROUP):
        o_ref[:, g * HEAD_DIM:(g + 1) * HEAD_DIM] = o[:, g * bq:(g + 1) * bq].T.astype(BF16)


def _dense_attention(rows, qt, k, vt, kx, vxt, prev):
    nkv = k.shape[0]
    bq, bk = 256, 512
    s = rows.s_lat
    assert s % bk == 0 and s % bq == 0
    nqb = s // bq
    out0 = rows.tc // bq
    n_ctx = kx.shape[2]
    return pl.pallas_call(
        functools.partial(_dense_attn_kernel, bq=bq, bk=bk),
        grid=(rows.n_lat_seq, nkv, nqb),
        in_specs=[pl.BlockSpec((GQA_GROUP, HEAD_DIM, bq), lambda b, h, i: (h, 0, b * nqb + i)),
                  pl.BlockSpec((1, s, HEAD_DIM), lambda b, h, i: (h, b, 0)),
                  pl.BlockSpec((1, HEAD_DIM, s), lambda b, h, i: (h, 0, b)),
                  pl.BlockSpec((1, 1, n_ctx, HEAD_DIM), lambda b, h, i: (b, h, 0, 0)),
                  pl.BlockSpec((1, 1, HEAD_DIM, n_ctx), lambda b, h, i: (b, h, 0, 0)),
                  pl.BlockSpec(memory_space=pl.ANY)],
        out_specs=pl.BlockSpec((bq, GQA_GROUP * HEAD_DIM), lambda b, h, i: (out0 + b * nqb + i, h)),
        out_shape=jax.ShapeDtypeStruct(prev.shape, BF16),
        input_output_aliases={5: 0},
        compiler_params=_cparams(("arbitrary", "arbitrary", "arbitrary"), 48),
        name="dense_attention",
    )(qt, k, vt, kx, vxt, prev)


def _dft_tables(n):
    j = jnp.arange(n, dtype=jnp.int32)
    jk = (j[:, None] * j[None, :]) % n
    ang = jk.astype(F32) * (2.0 * math.pi / n)
    scale = n ** -0.5
    return (jnp.cos(ang) * scale).astype(BF16), (jnp.sin(ang) * scale).astype(BF16)


def _chan_dft_kernel(a_ref, cs_ref, uc_ref, us_ref, *, gd):
    for g in range(A_GROUPS):
        sl = slice(g * gd, (g + 1) * gd)
        y = _dot(a_ref[:, sl].astype(BF16), cs_ref[...])
        uc_ref[:, sl] = y[:, :gd].astype(BF16)
        us_ref[:, sl] = y[:, gd:].astype(BF16)


def _chan_dft(proj, a_width, cs):
    t = proj.shape[0]
    gd = a_width // A_GROUPS
    tm = ROW_TILE
    return pl.pallas_call(
        functools.partial(_chan_dft_kernel, gd=gd),
        grid=(t // tm,),
        in_specs=[pl.BlockSpec((tm, a_width), lambda i: (i, 0)),
                  pl.BlockSpec((gd, 2 * gd), lambda i: (0, 0))],
        out_specs=[pl.BlockSpec((tm, a_width), lambda i: (i, 0)),
                   pl.BlockSpec((tm, a_width), lambda i: (i, 0))],
        out_shape=[jax.ShapeDtypeStruct((t, a_width), BF16)] * 2,
        compiler_params=_cparams(("arbitrary",), 40),
        name="channel_dft",
    )(proj, cs)


def _seq_dft_kernel(c_ref, s_ref, uc_ref, us_ref, *rest):
    o_ref = rest[-1]
    o_ref[...] = (_dot(c_ref[...], uc_ref[...]) - _dot(s_ref[...], us_ref[...])).astype(BF16)


def _seq_dft(uc, us, cmat, smat, row0, n_seq, prev):
    t, width = uc.shape
    s = cmat.shape[0]
    tm = min(s, ROW_TILE)
    tn = 512
    ni = s // tm
    seq0 = row0 // s
    out0 = row0 // tm
    in_specs = [pl.BlockSpec((tm, s), lambda b, j, i: (i, 0)),
                pl.BlockSpec((tm, s), lambda b, j, i: (i, 0)),
                pl.BlockSpec((s, tn), lambda b, j, i: (seq0 + b, j)),
                pl.BlockSpec((s, tn), lambda b, j, i: (seq0 + b, j))]
    args = [cmat, smat, uc, us]
    aliases = {}
    if prev is not None:
        in_specs.append(pl.BlockSpec(memory_space=pl.ANY))
        args.append(prev)
        aliases = {4: 0}
    return pl.pallas_call(
        _seq_dft_kernel,
        grid=(n_seq, width // tn, ni),
        in_specs=in_specs,
        out_specs=pl.BlockSpec((tm, tn), lambda b, j, i: (out0 + b * ni + i, j)),
        out_shape=jax.ShapeDtypeStruct((t, width), BF16),
        input_output_aliases=aliases,
        compiler_params=_cparams(("arbitrary", "arbitrary", "arbitrary"), 48),
        name="sequence_dft",
    )(*args)


def _out_proj_kernel(*refs, n_lhs):
    lhs = refs[:n_lhs]
    ws = refs[n_lhs:2 * n_lhs]
    x_ref, mod_ref, o_ref = refs[2 * n_lhs:]
    acc = None
    for a, w in zip(lhs, ws):
        part = _dot(a[...], w[...])
        acc = part if acc is None else acc + part
    o_ref[...] = x_ref[...] + mod_ref[0, 2:3, :] * acc


def _out_proj(rows, lhs_list, w_bf16, x, mod_l):
    t, d = x.shape
    tm, tn = ROW_TILE, 512
    in_specs, args, w_args, w_specs = [], [], [], []
    k0 = 0
    for a in lhs_list:
        kw = a.shape[1]
        in_specs.append(pl.BlockSpec((tm, kw), lambda i, j: (i, 0)))
        args.append(a)
        kb = k0 // kw
        assert k0 % kw == 0
        w_specs.append(pl.BlockSpec((kw, tn), lambda i, j, kb=kb: (kb, j)))
        w_args.append(w_bf16)
        k0 += kw
    in_specs += w_specs + [pl.BlockSpec((tm, tn), lambda i, j: (i, j)),
                           pl.BlockSpec((1, 6, tn), lambda i, j: (rows.mod_row(i, tm), 0, j))]
    args += w_args + [x, mod_l]
    return pl.pallas_call(
        functools.partial(_out_proj_kernel, n_lhs=len(lhs_list)),
        grid=(t // tm, d // tn),
        in_specs=in_specs,
        out_specs=pl.BlockSpec((tm, tn), lambda i, j: (i, j)),
        out_shape=jax.ShapeDtypeStruct((t, d), F32),
        compiler_params=_cparams(("arbitrary", "arbitrary"), 40),
        name="out_proj",
    )(*args)


def _ffn_norm_kernel(x_ref, mod_ref, g_ref, whi_ref, wlo_ref, b_ref, h_ref, lg_ref):
    h = _norm_mod(x_ref[...], g_ref[...], mod_ref[0, 4:5, :], mod_ref[0, 3:4, :])
    hi = h.astype(BF16)
    lo = (h - hi.astype(F32)).astype(BF16)
    h_ref[...] = h
    lg_ref[...] = (_dot(hi, whi_ref[...]) + _dot(hi, wlo_ref[...]) + _dot(lo, whi_ref[...])
                   + b_ref[...])


def _ffn_norm_router(rows, x, mod_l, g, w_hi, w_lo, b_r):
    t, d = x.shape
    tm = ROW_TILE
    return pl.pallas_call(
        _ffn_norm_kernel,
        grid=(t // tm,),
        in_specs=[pl.BlockSpec((tm, d), lambda i: (i, 0)),
                  pl.BlockSpec((1, 6, d), lambda i: (rows.mod_row(i, tm), 0, 0)),
                  pl.BlockSpec((1, d), lambda i: (0, 0)),
                  pl.BlockSpec((d, LOGIT_LANES), lambda i: (0, 0)),
                  pl.BlockSpec((d, LOGIT_LANES), lambda i: (0, 0)),
                  pl.BlockSpec((1, LOGIT_LANES), lambda i: (0, 0))],
        out_specs=[pl.BlockSpec((tm, d), lambda i: (i, 0)),
                   pl.BlockSpec((tm, LOGIT_LANES), lambda i: (i, 0))],
        out_shape=[jax.ShapeDtypeStruct((t, d), F32),
                   jax.ShapeDtypeStruct((t, LOGIT_LANES), F32)],
        compiler_params=_cparams(("arbitrary",), 40),
        name="ffn_norm_router",
    )(x, mod_l, g.reshape(1, d), w_hi, w_lo, b_r)


def _expert_kernel(be_ref, nb_ref, idx_ref, nxt_ref, h_hbm, wg_ref, wu_ref, wd_ref, y_hbm,
                   xbuf, ybuf, wgb, wub, wdb, gsem, ssem, *, n_tok):
    b = pl.program_id(0)
    n_used = nb_ref[0]
    bm = xbuf.shape[1]
    slot = b % 2

    def gather_copy(ref, r, dst_slot):
        src = jnp.minimum(ref[0, 0, r] // TOP_K, n_tok - 1)
        return pltpu.make_async_copy(h_hbm.at[pl.ds(src, 1)], xbuf.at[dst_slot, pl.ds(r, 1)],
                                     gsem.at[dst_slot])

    def scatter_copy(r):
        return pltpu.make_async_copy(ybuf.at[pl.ds(r, 1)], y_hbm.at[pl.ds(idx_ref[0, 0, r], 1)],
                                     ssem.at[0])

    def for_rows(fn):
        def body(r, carry):
            fn(r)
            return carry
        lax.fori_loop(0, bm, body, 0, unroll=8)

    @pl.when(b == 0)
    def _():
        for_rows(lambda r: gather_copy(idx_ref, r, 0).start())

    @pl.when(b < n_used)
    def _():
        @pl.when(b + 1 < n_used)
        def _():
            for_rows(lambda r: gather_copy(nxt_ref, r, 1 - slot).start())

        for_rows(lambda r: gather_copy(idx_ref, r, slot).wait())

        @pl.when((b == 0) | (be_ref[b] != be_ref[jnp.maximum(b - 1, 0)]))
        def _():
            wgb[...] = wg_ref[0, 0].astype(BF16)
            wub[...] = wu_ref[0, 0].astype(BF16)
            wdb[...] = wd_ref[0, 0].astype(BF16)

        x = xbuf[slot].astype(BF16)
        hid = (_silu(_dot(x, wgb[...])) * _dot(x, wub[...])).astype(BF16)

        @pl.when(b > 0)
        def _():
            for_rows(lambda r: scatter_copy(r).wait())

        ybuf[...] = _dot(hid, wdb[...])
        for_rows(lambda r: scatter_copy(r).start())

        @pl.when(b == n_used - 1)
        def _():
            for_rows(lambda r: scatter_copy(r).wait())


def _experts(h, idx, blk_expert, n_used, layer, w_gate, w_up, w_down):
    t, d = h.shape
    de = w_gate.shape[3]
    bm = MOE_ROWS
    nblk = idx.shape[0]

    def w_map(b, be, nb):
        return (layer, be[b], 0, 0)

    grid_spec = pltpu.PrefetchScalarGridSpec(
        num_scalar_prefetch=2,
        grid=(nblk,),
        in_specs=[pl.BlockSpec((1, 1, bm), lambda b, be, nb: (b, 0, 0), memory_space=pltpu.SMEM),
                  pl.BlockSpec((1, 1, bm), lambda b, be, nb: (jnp.minimum(b + 1, nblk - 1), 0, 0),
                               memory_space=pltpu.SMEM),
                  pl.BlockSpec(memory_space=pl.ANY),
                  pl.BlockSpec((1, 1, d, de), w_map),
                  pl.BlockSpec((1, 1, d, de), w_map),
                  pl.BlockSpec((1, 1, de, d), w_map)],
        out_specs=pl.BlockSpec(memory_space=pl.ANY),
        scratch_shapes=[pltpu.VMEM((2, bm, d), F32), pltpu.VMEM((bm, d), F32),
                        pltpu.VMEM((d, de), BF16), pltpu.VMEM((d, de), BF16),
                        pltpu.VMEM((de, d), BF16),
                        pltpu.SemaphoreType.DMA((2,)), pltpu.SemaphoreType.DMA((1,))],
    )
    return pl.pallas_call(
        functools.partial(_expert_kernel, n_tok=t),
        grid_spec=grid_spec,
        out_shape=jax.ShapeDtypeStruct((t * TOP_K + bm, d), F32),
        compiler_params=_cparams(("arbitrary",), 56),
        name="moe_experts",
    )(blk_expert, n_used, idx, idx, h, w_gate, w_up, w_down)


def _combine_kernel(x_ref, y_ref, w_ref, mod_ref, o_ref):
    d = x_ref.shape[1]
    ff = w_ref[:, 0:1] * y_ref[:, :d] + w_ref[:, 1:2] * y_ref[:, d:]
    o_ref[...] = x_ref[...] + mod_ref[0, 5:6, :] * ff


def _combine(rows, x, y_pairs, wts, mod_l):
    t, d = x.shape
    tm = 256
    return pl.pallas_call(
        _combine_kernel,
        grid=(t // tm,),
        in_specs=[pl.BlockSpec((tm, d), lambda i: (i, 0)),
                  pl.BlockSpec((tm, TOP_K * d), lambda i: (i, 0)),
                  pl.BlockSpec((tm, TOP_K), lambda i: (i, 0)),
                  pl.BlockSpec((1, 6, d), lambda i: (rows.mod_row(i, tm), 0, 0))],
        out_specs=pl.BlockSpec((tm, d), lambda i: (i, 0)),
        out_shape=jax.ShapeDtypeStruct((t, d), F32),
        compiler_params=_cparams(("arbitrary",), 40),
        name="moe_combine",
    )(x, y_pairs, wts, mod_l)


def _moe(rows, x, mod_l, g, layer, w_group, b_group, w_route, b_route, w_gate, w_up, w_down):
    t, d = x.shape
    n_exp = w_route.shape[1]
    per_group = n_exp // N_GROUPS
    pad = LOGIT_LANES - N_GROUPS - n_exp
    w_r = jnp.concatenate([w_group, w_route, jnp.zeros((d, pad), F32)], axis=1)
    b_r = jnp.concatenate([b_group, b_route, jnp.zeros((pad,), F32)]).reshape(1, LOGIT_LANES)
    w_hi = w_r.astype(BF16)
    w_lo = (w_r - w_hi.astype(F32)).astype(BF16)
    h, logits = _ffn_norm_router(rows, x, mod_l, g, w_hi, w_lo, b_r)

    g_logit = logits[:, :N_GROUPS]
    g_idx = jnp.argmax(g_logit, axis=-1).astype(jnp.int32)
    g_w = 1.0 / jnp.sum(jnp.exp(g_logit - jnp.max(g_logit, axis=-1, keepdims=True)), axis=-1, keepdims=True)
    e_all = logits[:, N_GROUPS:N_GROUPS + n_exp].reshape(t, N_GROUPS, per_group)
    e_logit = jnp.take_along_axis(e_all, g_idx[:, None, None], axis=1)[:, 0]
    top_p, top_i = lax.top_k(jax.nn.softmax(e_logit, axis=-1), TOP_K)
    wts = g_w * top_p / jnp.sum(top_p, axis=-1, keepdims=True)
    eid = (g_idx[:, None] * per_group + top_i).reshape(-1).astype(jnp.int32)

    bm = MOE_ROWS
    n_assign = t * TOP_K
    order = jnp.argsort(eid).astype(jnp.int32)
    e_sorted = eid[order]
    bounds = jnp.searchsorted(e_sorted, jnp.arange(n_exp + 1, dtype=jnp.int32), side='left').astype(jnp.int32)
    start, sizes = bounds[:-1], bounds[1:] - bounds[:-1]
    padded = ((sizes + bm - 1) // bm) * bm
    pad_end = jnp.cumsum(padded)
    pad_start = pad_end - padded
    nblk = -(-(n_assign + n_exp * (bm - 1)) // bm)
    blk_start = jnp.arange(nblk, dtype=jnp.int32) * bm
    blk_expert = jnp.minimum(jnp.searchsorted(pad_end, blk_start, side='right'), n_exp - 1).astype(jnp.int32)
    n_used = (pad_end[-1:] // bm).astype(jnp.int32)
    blk_off = blk_start - pad_start[blk_expert]
    n_valid = jnp.clip(sizes[blk_expert] - blk_off, 0, bm)
    r = jnp.arange(bm, dtype=jnp.int32)[None, :]
    src = jnp.clip(start[blk_expert][:, None] + blk_off[:, None] + r, 0, n_assign - 1)
    idx = jnp.where(r < n_valid[:, None], order[src], n_assign + r).astype(jnp.int32)
    ys = _experts(h, idx.reshape(nblk, 1, bm), blk_expert, n_used, layer, w_gate, w_up, w_down)
    y_pairs = ys.reshape((n_assign + bm) // TOP_K, TOP_K * d)
    return _combine(rows, x, y_pairs, wts, mod_l)


def _rope_tables(s_lat):
    n_rows = s_lat // GRID_W
    row = jnp.repeat(jnp.arange(n_rows, dtype=F32), GRID_W)
    col = jnp.tile(jnp.arange(GRID_W, dtype=F32), n_rows)
    axis_dim = HEAD_DIM // 2
    inv_freq = ROPE_THETA ** (-jnp.arange(0, axis_dim, 2, dtype=F32) / axis_dim)
    ar, ac = row[:, None] * inv_freq, col[:, None] * inv_freq
    cos = jnp.concatenate([jnp.cos(ar), jnp.cos(ar), jnp.cos(ac), jnp.cos(ac)], axis=1)
    sin = jnp.concatenate([-jnp.sin(ar), jnp.sin(ar), -jnp.sin(ac), jnp.sin(ac)], axis=1)
    return cos, sin


def _cache_heads(c, transposed=False):
    perm = (0, 2, 3, 1) if transposed else (0, 2, 1, 3)
    return jnp.transpose(c, perm).astype(BF16)


def kernel(x_prompt, x_sample, cache_b_k, cache_b_v, cache_c_k, cache_c_v, c, c_ctx,
           w_mod, b_mod, norm_mix, norm_ffn, w_in_even, w_out_even, q_norm_b, k_norm_b, sink_b,
           w_in_odd, w_out_odd, q_norm_c, k_norm_c, w_group, b_group, w_route, b_route,
           w_gate, w_up, w_down):
    n_ctx_seq, s_ctx, d = x_prompt.shape
    n_lat_seq, s_lat, _ = x_sample.shape
    depth = w_mod.shape[0]
    rows = _Rows(n_ctx_seq, s_ctx, n_lat_seq, s_lat)
    assert 1 + n_lat_seq <= MOD_ROWS and rows.tc % ROW_TILE == 0 and s_lat % ROW_TILE == 0
    a_width = A_GROUPS * (d // 8)
    b_heads = (d - a_width) // HEAD_DIM
    b_kv = b_heads // GQA_GROUP
    c_heads = d // HEAD_DIM
    c_kv = c_heads // GQA_GROUP
    q_scale = HEAD_DIM ** -0.5

    cvec = jnp.concatenate([c_ctx[None, :], c, jnp.zeros((MOD_ROWS - 1 - n_lat_seq, d), F32)], axis=0)
    mod = _modulation(cvec, w_mod, b_mod).reshape(depth, MOD_ROWS, 6, d)

    x = jnp.concatenate([x_prompt.reshape(rows.tc, d), x_sample.reshape(rows.tl, d)], axis=0)
    rope = _rope_tables(s_lat)
    gd = a_width // A_GROUPS
    cc, sc = _dft_tables(gd)
    chan_cs = jnp.concatenate([cc, sc], axis=1)
    dft_ctx = _dft_tables(s_ctx)
    dft_lat = _dft_tables(s_lat)

    new_kv = []
    for l in range(depth):
        i = l // 2
        mod_l = mod[l]
        if l % 2 == 0:
            proj = _in_proj(rows, x, mod_l, norm_mix[l], w_in_even[i].astype(BF16))
            qc, kc, vc, kf, vf = _prep(proj, a_width, b_heads, b_kv, q_norm_b[i], k_norm_b[i],
                                       0, rows.tc, None, True, False, q_scale)
            q, k, v = _prep(proj, a_width, b_heads, b_kv, q_norm_b[i], k_norm_b[i],
                            rows.tc, rows.tl, rope, False, False, q_scale)
            sink = jnp.broadcast_to(sink_b[i][:, None, None], (b_heads, 1, HEAD_DIM))
            att = _ctx_attention(rows, qc, kc, vc, sink, b_heads * HEAD_DIM)
            att = _window_attention(rows, q, k, v, _cache_heads(cache_b_k[:, i]),
                                    _cache_heads(cache_b_v[:, i]), sink, att)
            uc, us = _chan_dft(proj, a_width, chan_cs)
            fm = _seq_dft(uc, us, dft_ctx[0], dft_ctx[1], 0, n_ctx_seq, None)
            fm = _seq_dft(uc, us, dft_lat[0], dft_lat[1], rows.tc, n_lat_seq, fm)
            x = _out_proj(rows, [fm, att], w_out_even[i].astype(BF16), x, mod_l)
            nkv = b_kv
        else:
            proj = _in_proj(rows, x, mod_l, norm_mix[l], w_in_odd[i].astype(BF16))
            qc, kc, vc, kf, vf = _prep(proj, 0, c_heads, c_kv, q_norm_c[i], k_norm_c[i],
                                       0, rows.tc, None, True, False, q_scale)
            qt, k, vt = _prep(proj, 0, c_heads, c_kv, q_norm_c[i], k_norm_c[i],
                              rows.tc, rows.tl, rope, False, True, q_scale * LOG2E)
            att = _ctx_attention(rows, qc, kc, vc, None, c_heads * HEAD_DIM)
            att = _dense_attention(rows, qt, k, vt, _cache_heads(cache_c_k[:, i]),
                                   _cache_heads(cache_c_v[:, i], True), att)
            x = _out_proj(rows, [att], w_out_odd[i].astype(BF16), x, mod_l)
            nkv = c_kv
        new_kv.append((kf.reshape(n_ctx_seq, s_ctx, nkv, HEAD_DIM),
                       vf.reshape(n_ctx_seq, s_ctx, nkv, HEAD_DIM)))
        x = _moe(rows, x, mod_l, norm_ffn[l], l, w_group[l], b_group[l], w_route[l], b_route[l],
                 w_gate, w_up, w_down)

    y_prompt = x[:rows.tc].reshape(n_ctx_seq, s_ctx, d)
    y_sample = x[rows.tc:].reshape(n_lat_seq, s_lat, d)
    new_b_k = jnp.stack([new_kv[l][0] for l in range(0, depth, 2)], axis=1)
    new_b_v = jnp.stack([new_kv[l][1] for l in range(0, depth, 2)], axis=1)
    new_c_k = jnp.stack([new_kv[l][0] for l in range(1, depth, 2)], axis=1)
    new_c_v = jnp.stack([new_kv[l][1] for l in range(1, depth, 2)], axis=1)
    return (y_prompt, y_sample, new_b_k, new_b_v, new_c_k, new_c_v)
```

```python
import functools
import math

import jax
import jax.numpy as jnp
from jax import lax
from jax.experimental import pallas as pl
from jax.experimental.pallas import tpu as pltpu

F32 = jnp.float32
BF16 = jnp.bfloat16

HEAD_DIM = 128
GRID_W = 64
ROPE_THETA = 10000.0
WINDOW = 128
RMS_EPS = 1e-6
NEG_INF = -1e30
A_GROUPS = 4
N_GROUPS = 8
TOP_K = 2
GQA_GROUP = 4
MOD_ROWS = 8
LOGIT_LANES = 128
ROW_TILE = 512
MOE_ROWS = 256
LOG2E = math.log2(math.e)
MIB = 1024 * 1024


def _cparams(sem, vmem_mib):
    return pltpu.CompilerParams(dimension_semantics=sem, vmem_limit_bytes=vmem_mib * MIB)


def _silu(x):
    return x / (1.0 + jnp.exp(-x))


def _dot(a, b):
    return jnp.dot(a, b, preferred_element_type=F32)


def _dot_t(a, b):
    return lax.dot_general(a, b, (((1,), (1,)), ((), ())), preferred_element_type=F32)


def _mod_kernel(c_ref, w_ref, b_ref, o_ref):
    s = _silu(c_ref[...]).astype(BF16)
    o_ref[0] = _dot(s, w_ref[0].astype(BF16)) + b_ref[0]


def _modulation(cvec, w_mod, b_mod):
    depth, d, n6 = w_mod.shape
    tn = 1024
    return pl.pallas_call(
        _mod_kernel,
        grid=(depth, n6 // tn),
        in_specs=[pl.BlockSpec((MOD_ROWS, d), lambda l, j: (0, 0)),
                  pl.BlockSpec((1, d, tn), lambda l, j: (l, 0, j)),
                  pl.BlockSpec((1, 1, tn), lambda l, j: (l, 0, j))],
        out_specs=pl.BlockSpec((1, MOD_ROWS, tn), lambda l, j: (l, 0, j)),
        out_shape=jax.ShapeDtypeStruct((depth, MOD_ROWS, n6), F32),
        compiler_params=_cparams(("arbitrary", "arbitrary"), 40),
        name="modulation",
    )(cvec, w_mod, b_mod.reshape(depth, 1, n6))


class _Rows:
    def __init__(self, n_ctx_seq, s_ctx, n_lat_seq, s_lat):
        self.n_ctx_seq, self.s_ctx, self.n_lat_seq, self.s_lat = n_ctx_seq, s_ctx, n_lat_seq, s_lat
        self.tc = n_ctx_seq * s_ctx
        self.tl = n_lat_seq * s_lat
        self.t = self.tc + self.tl

    def mod_row(self, i, tm):
        nct = self.tc // tm
        per = self.s_lat // tm
        return jnp.where(i < nct, 0, 1 + (i - nct) // per)


def _norm_mod(x, g, scale, shift):
    ms = jnp.mean(x * x, axis=-1, keepdims=True)
    y = x * lax.rsqrt(ms + RMS_EPS) * g
    return y * (1.0 + scale) + shift


def _in_proj_kernel(x_ref, mod_ref, g_ref, w_ref, o_ref, h_ref):
    i = pl.program_id(0)
    slot = i % 2

    @pl.when(i == 0)
    def _():
        h_ref[1] = jnp.zeros(h_ref.shape[1:], BF16)

    h = _norm_mod(x_ref[...], g_ref[...], mod_ref[0, 1:2, :], mod_ref[0, 0:1, :])
    o_ref[...] = _dot(h_ref[1 - slot], w_ref[...])
    h_ref[slot] = h.astype(BF16)


def _in_proj(rows, x, mod_l, g, w_bf16):
    t, d = x.shape
    n = w_bf16.shape[1]
    tm = ROW_TILE
    nt = t // tm

    def cur(i):
        return jnp.minimum(i, nt - 1)

    return pl.pallas_call(
        _in_proj_kernel,
        grid=(nt + 1,),
        in_specs=[pl.BlockSpec((tm, d), lambda i: (cur(i), 0)),
                  pl.BlockSpec((1, 6, d), lambda i: (rows.mod_row(cur(i), tm), 0, 0)),
                  pl.BlockSpec((1, d), lambda i: (0, 0)),
                  pl.BlockSpec((d, n), lambda i: (0, 0))],
        out_specs=pl.BlockSpec((tm, n), lambda i: (jnp.maximum(i - 1, 0), 0)),
        out_shape=jax.ShapeDtypeStruct((t, n), F32),
        scratch_shapes=[pltpu.VMEM((2, tm, d), BF16)],
        compiler_params=_cparams(("arbitrary",), 56),
        name="in_proj",
    )(x, mod_l, g.reshape(1, d), w_bf16)


def _swap32(y):
    lane = lax.broadcasted_iota(jnp.int32, y.shape, 1)
    return jnp.where((lane & 63) < 32, pltpu.roll(y, 96, 1), pltpu.roll(y, 32, 1))


def _prep_kernel(*refs, nq, nkv, rope, emit_f32, transposed, q_scale):
    q_ref, k_ref, v_ref, qn_ref, kn_ref = refs[:5]
    pos = 5
    if rope:
        cos_ref, sin_ref = refs[pos:pos + 2]
        pos += 2
    qo_ref, ko_ref, vo_ref = refs[pos:pos + 3]
    pos += 3
    if emit_f32:
        kf_ref, vf_ref = refs[pos:pos + 2]

    def head_norm(x, gain):
        ms = jnp.mean(x * x, axis=-1, keepdims=True)
        return x * lax.rsqrt(ms + RMS_EPS) * gain

    def rot(y):
        if not rope:
            return y
        return y * cos_ref[...] + _swap32(y) * sin_ref[...]

    for h in range(nq):
        sl = slice(h * HEAD_DIM, (h + 1) * HEAD_DIM)
        y = rot(head_norm(q_ref[:, sl], qn_ref[...])) * q_scale
        qo_ref[h] = (y.T if transposed else y).astype(BF16)
    for h in range(nkv):
        sl = slice(h * HEAD_DIM, (h + 1) * HEAD_DIM)
        y = head_norm(k_ref[:, sl], kn_ref[...])
        v = v_ref[:, sl]
        if emit_f32:
            kf_ref[:, sl] = y
            vf_ref[:, sl] = v
        ko_ref[h] = rot(y).astype(BF16)
        vo_ref[h] = (v.T if transposed else v).astype(BF16)


def _prep(proj, col0, nq, nkv, qn, kn, row0, n_rows, rope_tabs, emit_f32, transposed, q_scale):
    tm = 256
    rope = rope_tabs is not None
    r0 = row0 // tm
    qw, kw = nq * HEAD_DIM, nkv * HEAD_DIM
    qb, kb, vb = col0 // qw, (col0 + qw) // kw, (col0 + qw + kw) // kw
    assert col0 % qw == 0 and (col0 + qw) % kw == 0 and row0 % tm == 0
    in_specs = [pl.BlockSpec((tm, qw), lambda i: (r0 + i, qb)),
                pl.BlockSpec((tm, kw), lambda i: (r0 + i, kb)),
                pl.BlockSpec((tm, kw), lambda i: (r0 + i, vb)),
                pl.BlockSpec((1, HEAD_DIM), lambda i: (0, 0)),
                pl.BlockSpec((1, HEAD_DIM), lambda i: (0, 0))]
    args = [proj, proj, proj, qn.reshape(1, HEAD_DIM), kn.reshape(1, HEAD_DIM)]
    if rope:
        per = rope_tabs[0].shape[0] // tm
        in_specs += [pl.BlockSpec((tm, HEAD_DIM), lambda i: (i % per, 0)),
                     pl.BlockSpec((tm, HEAD_DIM), lambda i: (i % per, 0))]
        args += list(rope_tabs)
    row_major = pl.BlockSpec((nq, tm, HEAD_DIM), lambda i: (0, i, 0))
    col_major = pl.BlockSpec((nq, HEAD_DIM, tm), lambda i: (0, 0, i))
    kv_row = pl.BlockSpec((nkv, tm, HEAD_DIM), lambda i: (0, i, 0))
    kv_col = pl.BlockSpec((nkv, HEAD_DIM, tm), lambda i: (0, 0, i))
    if transposed:
        out_specs = [col_major, kv_row, kv_col]
        out_shape = [jax.ShapeDtypeStruct((nq, HEAD_DIM, n_rows), BF16),
                     jax.ShapeDtypeStruct((nkv, n_rows, HEAD_DIM), BF16),
                     jax.ShapeDtypeStruct((nkv, HEAD_DIM, n_rows), BF16)]
    else:
        out_specs = [row_major, kv_row, kv_row]
        out_shape = [jax.ShapeDtypeStruct((nq, n_rows, HEAD_DIM), BF16),
                     jax.ShapeDtypeStruct((nkv, n_rows, HEAD_DIM), BF16),
                     jax.ShapeDtypeStruct((nkv, n_rows, HEAD_DIM), BF16)]
    if emit_f32:
        out_specs += [pl.BlockSpec((tm, kw), lambda i: (i, 0)),
                      pl.BlockSpec((tm, kw), lambda i: (i, 0))]
        out_shape += [jax.ShapeDtypeStruct((n_rows, kw), F32),
                      jax.ShapeDtypeStruct((n_rows, kw), F32)]
    return pl.pallas_call(
        functools.partial(_prep_kernel, nq=nq, nkv=nkv, rope=rope, emit_f32=emit_f32,
                          transposed=transposed, q_scale=q_scale),
        grid=(n_rows // tm,),
        in_specs=in_specs,
        out_specs=out_specs,
        out_shape=out_shape,
        compiler_params=_cparams(("arbitrary",), 40),
        name="qkv_prep_rope" if rope else "qkv_prep",
    )(*args)


def _softmax_pv(s, v_list, widths, sink):
    m = jnp.max(s, axis=-1, keepdims=True)
    if sink is not None:
        m = jnp.maximum(m, sink)
    p = jnp.exp(s - m)
    l = jnp.sum(p, axis=-1, keepdims=True)
    if sink is not None:
        l = l + jnp.exp(sink - m)
    pb = p.astype(BF16)
    o = None
    c0 = 0
    for v, w in zip(v_list, widths):
        part = _dot(pb[:, c0:c0 + w], v)
        o = part if o is None else o + part
        c0 += w
    return o / l


def _ctx_attn_kernel(*refs, has_sink):
    if has_sink:
        q_ref, k_ref, v_ref, sink_ref, o_ref = refs
    else:
        q_ref, k_ref, v_ref, o_ref = refs
    k = k_ref[0]
    v = v_ref[0]
    for g in range(GQA_GROUP):
        s = _dot_t(q_ref[g], k)
        sink = sink_ref[g][:, 0:1] if has_sink else None
        o = _softmax_pv(s, [v], [k.shape[0]], sink)
        o_ref[:, g * HEAD_DIM:(g + 1) * HEAD_DIM] = o.astype(BF16)


def _ctx_attention(rows, q, k, v, sink, att_cols):
    nkv = k.shape[0]
    s = rows.s_ctx
    gw = GQA_GROUP * HEAD_DIM
    has_sink = sink is not None
    in_specs = [pl.BlockSpec((GQA_GROUP, s, HEAD_DIM), lambda b, h: (h, b, 0)),
                pl.BlockSpec((1, s, HEAD_DIM), lambda b, h: (h, b, 0)),
                pl.BlockSpec((1, s, HEAD_DIM), lambda b, h: (h, b, 0))]
    args = [q, k, v]
    if has_sink:
        in_specs.append(pl.BlockSpec((GQA_GROUP, 1, HEAD_DIM), lambda b, h: (h, 0, 0)))
        args.append(sink)
    return pl.pallas_call(
        functools.partial(_ctx_attn_kernel, has_sink=has_sink),
        grid=(rows.n_ctx_seq, nkv),
        in_specs=in_specs,
        out_specs=pl.BlockSpec((s, gw), lambda b, h: (b, h)),
        out_shape=jax.ShapeDtypeStruct((rows.t, att_cols), BF16),
        compiler_params=_cparams(("arbitrary", "arbitrary"), 40),
        name="ctx_attention",
    )(*args)


def _window_attn_kernel(q_ref, kp_ref, kc_ref, kn_ref, vp_ref, vc_ref, vn_ref, kx_ref, vx_ref,
                        sink_ref, prev_ref, o_ref, *, nkv, n_blocks):
    del prev_ref
    n = pl.program_id(1)
    qb = WINDOW
    m_rows = GQA_GROUP * qb
    r = lax.broadcasted_iota(jnp.int32, (m_rows, qb), 0) & (qb - 1)
    c = lax.broadcasted_iota(jnp.int32, (m_rows, qb), 1)
    ok_prev = (c >= r) & (n > 0)
    ok_next = (c <= r) & (n < n_blocks - 1)
    n_ctx = kx_ref.shape[2]
    for h in range(nkv):
        q = q_ref[h * GQA_GROUP:(h + 1) * GQA_GROUP].reshape(m_rows, HEAD_DIM)
        s = jnp.concatenate(
            [jnp.where(ok_prev, _dot_t(q, kp_ref[h]), NEG_INF),
             _dot_t(q, kc_ref[h]),
             jnp.where(ok_next, _dot_t(q, kn_ref[h]), NEG_INF),
             _dot_t(q, kx_ref[0, h])], axis=-1)
        sink = jnp.concatenate(
            [jnp.broadcast_to(sink_ref[h * GQA_GROUP + g][:, 0:1], (qb, 1)) for g in range(GQA_GROUP)],
            axis=0)
        o = _softmax_pv(s, [vp_ref[h], vc_ref[h], vn_ref[h], vx_ref[0, h]], [qb, qb, qb, n_ctx], sink)
        for g in range(GQA_GROUP):
            col = (h * GQA_GROUP + g) * HEAD_DIM
            o_ref[:, col:col + HEAD_DIM] = o[g * qb:(g + 1) * qb].astype(BF16)


def _window_attention(rows, q, k, v, kx, vx, sink, prev):
    nq, nkv = q.shape[0], k.shape[0]
    qb = WINDOW
    nb = rows.s_lat // qb
    out0 = rows.tc // qb
    n_ctx = kx.shape[2]

    def kv_spec(shift):
        def imap(b, n):
            return (0, b * nb + jnp.clip(n + shift, 0, nb - 1), 0)
        return pl.BlockSpec((nkv, qb, HEAD_DIM), imap)

    cache_spec = pl.BlockSpec((1, nkv, n_ctx, HEAD_DIM), lambda b, n: (b, 0, 0, 0))
    return pl.pallas_call(
        functools.partial(_window_attn_kernel, nkv=nkv, n_blocks=nb),
        grid=(rows.n_lat_seq, nb),
        in_specs=[pl.BlockSpec((nq, qb, HEAD_DIM), lambda b, n: (0, b * nb + n, 0)),
                  kv_spec(-1), kv_spec(0), kv_spec(1),
                  kv_spec(-1), kv_spec(0), kv_spec(1),
                  cache_spec, cache_spec,
                  pl.BlockSpec((nq, 1, HEAD_DIM), lambda b, n: (0, 0, 0)),
                  pl.BlockSpec(memory_space=pl.ANY)],
        out_specs=pl.BlockSpec((qb, nq * HEAD_DIM), lambda b, n: (out0 + b * nb + n, 0)),
        out_shape=jax.ShapeDtypeStruct(prev.shape, BF16),
        input_output_aliases={10: 0},
        compiler_params=_cparams(("arbitrary", "arbitrary"), 40),
        name="window_attention",
    )(q, k, k, k, v, v, v, kx, vx, sink, prev)


def _dense_attn_kernel(qt_ref, k_ref, vt_ref, kx_ref, vxt_ref, prev_ref, o_ref, *, bq, bk):
    del prev_ref
    m_cols = GQA_GROUP * bq
    qt = jnp.concatenate([qt_ref[g] for g in range(GQA_GROUP)], axis=1)

    def update(k, vt, carry):
        m_old, l_old, acc = carry
        s = _dot(k, qt)
        m_new = jnp.maximum(m_old, jnp.max(s, axis=0, keepdims=True))
        alpha = jnp.exp2(m_old - m_new)
        p = jnp.exp2(s - m_new)
        l_new = alpha * l_old + jnp.sum(p, axis=0, keepdims=True)
        acc = alpha * acc + _dot(vt, p.astype(BF16))
        return m_new, l_new, acc

    carry = (jnp.full((1, m_cols), NEG_INF, F32), jnp.zeros((1, m_cols), F32),
             jnp.zeros((HEAD_DIM, m_cols), F32))
    for c in range(k_ref.shape[1] // bk):
        carry = update(k_ref[0, c * bk:(c + 1) * bk, :], vt_ref[0, :, c * bk:(c + 1) * bk], carry)
    _, l, acc = update(kx_ref[0, 0], vxt_ref[0, 0], carry)
    o = acc / l
    for g in range(GQA_GROUP):
        o_ref[:, g * HEAD_DIM:(g + 1) * HEAD_DIM] = o[:, g * bq:(g + 1) * bq].T.astype(BF16)


def _dense_attention(rows, qt, k, vt, kx, vxt, prev):
    nkv = k.shape[0]
    bq, bk = 256, 512
    s = rows.s_lat
    assert s % bk == 0 and s % bq == 0
    nqb = s // bq
    out0 = rows.tc // bq
    n_ctx = kx.shape[2]
    return pl.pallas_call(
        functools.partial(_dense_attn_kernel, bq=bq, bk=bk),
        grid=(rows.n_lat_seq, nkv, nqb),
        in_specs=[pl.BlockSpec((GQA_GROUP, HEAD_DIM, bq), lambda b, h, i: (h, 0, b * nqb + i)),
                  pl.BlockSpec((1, s, HEAD_DIM), lambda b, h, i: (h, b, 0)),
                  pl.BlockSpec((1, HEAD_DIM, s), lambda b, h, i: (h, 0, b)),
                  pl.BlockSpec((1, 1, n_ctx, HEAD_DIM), lambda b, h, i: (b, h, 0, 0)),
                  pl.BlockSpec((1, 1, HEAD_DIM, n_ctx), lambda b, h, i: (b, h, 0, 0)),
                  pl.BlockSpec(memory_space=pl.ANY)],
        out_specs=pl.BlockSpec((bq, GQA_GROUP * HEAD_DIM), lambda b, h, i: (out0 + b * nqb + i, h)),
        out_shape=jax.ShapeDtypeStruct(prev.shape, BF16),
        input_output_aliases={5: 0},
        compiler_params=_cparams(("arbitrary", "arbitrary", "arbitrary"), 48),
        name="dense_attention",
    )(qt, k, vt, kx, vxt, prev)


def _dft_tables(n):
    j = jnp.arange(n, dtype=jnp.int32)
    jk = (j[:, None] * j[None, :]) % n
    ang = jk.astype(F32) * (2.0 * math.pi / n)
    scale = n ** -0.5
    return (jnp.cos(ang) * scale).astype(BF16), (jnp.sin(ang) * scale).astype(BF16)


def _chan_dft_kernel(a_ref, cs_ref, uc_ref, us_ref, *, gd):
    for g in range(A_GROUPS):
        sl = slice(g * gd, (g + 1) * gd)
        y = _dot(a_ref[:, sl].astype(BF16), cs_ref[...])
        uc_ref[:, sl] = y[:, :gd].astype(BF16)
        us_ref[:, sl] = y[:, gd:].astype(BF16)


def _chan_dft(proj, a_width, cs):
    t = proj.shape[0]
    gd = a_width // A_GROUPS
    tm = ROW_TILE
    return pl.pallas_call(
        functools.partial(_chan_dft_kernel, gd=gd),
        grid=(t // tm,),
        in_specs=[pl.BlockSpec((tm, a_width), lambda i: (i, 0)),
                  pl.BlockSpec((gd, 2 * gd), lambda i: (0, 0))],
        out_specs=[pl.BlockSpec((tm, a_width), lambda i: (i, 0)),
                   pl.BlockSpec((tm, a_width), lambda i: (i, 0))],
        out_shape=[jax.ShapeDtypeStruct((t, a_width), BF16)] * 2,
        compiler_params=_cparams(("arbitrary",), 40),
        name="channel_dft",
    )(proj, cs)


def _seq_dft_kernel(c_ref, s_ref, uc_ref, us_ref, *rest):
    o_ref = rest[-1]
    o_ref[...] = (_dot(c_ref[...], uc_ref[...]) - _dot(s_ref[...], us_ref[...])).astype(BF16)


def _seq_dft(uc, us, cmat, smat, row0, n_seq, prev):
    t, width = uc.shape
    s = cmat.shape[0]
    tm = min(s, ROW_TILE)
    tn = 512
    ni = s // tm
    seq0 = row0 // s
    out0 = row0 // tm
    in_specs = [pl.BlockSpec((tm, s), lambda b, j, i: (i, 0)),
                pl.BlockSpec((tm, s), lambda b, j, i: (i, 0)),
                pl.BlockSpec((s, tn), lambda b, j, i: (seq0 + b, j)),
                pl.BlockSpec((s, tn), lambda b, j, i: (seq0 + b, j))]
    args = [cmat, smat, uc, us]
    aliases = {}
    if prev is not None:
        in_specs.append(pl.BlockSpec(memory_space=pl.ANY))
        args.append(prev)
        aliases = {4: 0}
    return pl.pallas_call(
        _seq_dft_kernel,
        grid=(n_seq, width // tn, ni),
        in_specs=in_specs,
        out_specs=pl.BlockSpec((tm, tn), lambda b, j, i: (out0 + b * ni + i, j)),
        out_shape=jax.ShapeDtypeStruct((t, width), BF16),
        input_output_aliases=aliases,
        compiler_params=_cparams(("arbitrary", "arbitrary", "arbitrary"), 48),
        name="sequence_dft",
    )(*args)


def _out_proj_kernel(*refs, n_lhs):
    lhs = refs[:n_lhs]
    ws = refs[n_lhs:2 * n_lhs]
    x_ref, mod_ref, o_ref = refs[2 * n_lhs:]
    acc = None
    for a, w in zip(lhs, ws):
        part = _dot(a[...], w[...])
        acc = part if acc is None else acc + part
    o_ref[...] = x_ref[...] + mod_ref[0, 2:3, :] * acc


def _out_proj(rows, lhs_list, w_bf16, x, mod_l):
    t, d = x.shape
    tm = ROW_TILE
    in_specs, args, w_args, w_specs = [], [], [], []
    k0 = 0
    for a in lhs_list:
        kw = a.shape[1]
        in_specs.append(pl.BlockSpec((tm, kw), lambda i: (i, 0)))
        args.append(a)
        kb = k0 // kw
        assert k0 % kw == 0
        w_specs.append(pl.BlockSpec((kw, d), lambda i, kb=kb: (kb, 0)))
        w_args.append(w_bf16)
        k0 += kw
    in_specs += w_specs + [pl.BlockSpec((tm, d), lambda i: (i, 0)),
                           pl.BlockSpec((1, 6, d), lambda i: (rows.mod_row(i, tm), 0, 0))]
    args += w_args + [x, mod_l]
    return pl.pallas_call(
        functools.partial(_out_proj_kernel, n_lhs=len(lhs_list)),
        grid=(t // tm,),
        in_specs=in_specs,
        out_specs=pl.BlockSpec((tm, d), lambda i: (i, 0)),
        out_shape=jax.ShapeDtypeStruct((t, d), F32),
        compiler_params=_cparams(("arbitrary",), 56),
        name="out_proj",
    )(*args)


def _ffn_norm_kernel(x_ref, mod_ref, g_ref, whi_ref, wlo_ref, b_ref, h_ref, lg_ref):
    h = _norm_mod(x_ref[...], g_ref[...], mod_ref[0, 4:5, :], mod_ref[0, 3:4, :])
    hi = h.astype(BF16)
    lo = (h - hi.astype(F32)).astype(BF16)
    h_ref[...] = h
    lg_ref[...] = (_dot(hi, whi_ref[...]) + _dot(hi, wlo_ref[...]) + _dot(lo, whi_ref[...])
                   + b_ref[...])


def _ffn_norm_router(rows, x, mod_l, g, w_hi, w_lo, b_r):
    t, d = x.shape
    tm = ROW_TILE
    return pl.pallas_call(
        _ffn_norm_kernel,
        grid=(t // tm,),
        in_specs=[pl.BlockSpec((tm, d), lambda i: (i, 0)),
                  pl.BlockSpec((1, 6, d), lambda i: (rows.mod_row(i, tm), 0, 0)),
                  pl.BlockSpec((1, d), lambda i: (0, 0)),
                  pl.BlockSpec((d, LOGIT_LANES), lambda i: (0, 0)),
                  pl.BlockSpec((d, LOGIT_LANES), lambda i: (0, 0)),
                  pl.BlockSpec((1, LOGIT_LANES), lambda i: (0, 0))],
        out_specs=[pl.BlockSpec((tm, d), lambda i: (i, 0)),
                   pl.BlockSpec((tm, LOGIT_LANES), lambda i: (i, 0))],
        out_shape=[jax.ShapeDtypeStruct((t, d), F32),
                   jax.ShapeDtypeStruct((t, LOGIT_LANES), F32)],
        compiler_params=_cparams(("arbitrary",), 40),
        name="ffn_norm_router",
    )(x, mod_l, g.reshape(1, d), w_hi, w_lo, b_r)


def _expert_kernel(be_ref, nb_ref, src_ref, nsrc_ref, dk_ref, dt_ref, h_hbm, wg_ref, wu_ref, wd_ref,
                   y_hbm, xbuf, ybuf, wgb, wub, wdb, gsem, ssem):
    b = pl.program_id(0)
    n_used = nb_ref[0]
    bm = xbuf.shape[1]
    slot = b % 2

    def gather_copy(ref, r, dst_slot):
        return pltpu.make_async_copy(h_hbm.at[pl.ds(ref[0, 0, r], 1)], xbuf.at[dst_slot, pl.ds(r, 1)],
                                     gsem.at[dst_slot])

    def scatter_copy(r):
        return pltpu.make_async_copy(ybuf.at[pl.ds(r, 1)],
                                     y_hbm.at[dk_ref[0, 0, r], pl.ds(dt_ref[0, 0, r], 1)], ssem.at[0])

    @pl.when(b == 0)
    def _():
        for r in range(bm):
            gather_copy(src_ref, r, 0).start()

    @pl.when(b < n_used)
    def _():
        for r in range(bm):
            gather_copy(src_ref, r, slot).wait()

        @pl.when((b == 0) | (be_ref[b] != be_ref[jnp.maximum(b - 1, 0)]))
        def _():
            wgb[...] = wg_ref[0, 0].astype(BF16)
            wub[...] = wu_ref[0, 0].astype(BF16)
            wdb[...] = wd_ref[0, 0].astype(BF16)

        for r in range(bm):
            gather_copy(nsrc_ref, r, 1 - slot).start()
        x = xbuf[slot].astype(BF16)
        hid = (_silu(_dot(x, wgb[...])) * _dot(x, wub[...])).astype(BF16)

        @pl.when(b > 0)
        def _():
            for r in range(bm):
                scatter_copy(r).wait()

        ybuf[...] = _dot(hid, wdb[...])
        for r in range(bm):
            scatter_copy(r).start()

        @pl.when(b == n_used - 1)
        def _():
            for r in range(bm):
                scatter_copy(r).wait()
            for r in range(bm):
                gather_copy(nsrc_ref, r, 1 - slot).wait()


def _experts(h, src, dst_k, dst_t, blk_expert, n_used, layer, w_gate, w_up, w_down):
    t, d = h.shape
    de = w_gate.shape[3]
    bm = MOE_ROWS
    nblk = src.shape[0]

    def w_map(b, be, nb):
        return (layer, be[b], 0, 0)

    def cur(b, be, nb):
        return (b, 0, 0)

    def nxt(b, be, nb):
        return (jnp.minimum(b + 1, jnp.maximum(nb[0] - 1, 0)), 0, 0)

    def smem(imap):
        return pl.BlockSpec((1, 1, bm), imap, memory_space=pltpu.SMEM)

    grid_spec = pltpu.PrefetchScalarGridSpec(
        num_scalar_prefetch=2,
        grid=(nblk,),
        in_specs=[smem(cur), smem(nxt), smem(cur), smem(cur),
                  pl.BlockSpec(memory_space=pl.ANY),
                  pl.BlockSpec((1, 1, d, de), w_map),
                  pl.BlockSpec((1, 1, d, de), w_map),
                  pl.BlockSpec((1, 1, de, d), w_map)],
        out_specs=pl.BlockSpec(memory_space=pl.ANY),
        scratch_shapes=[pltpu.VMEM((2, bm, d), F32), pltpu.VMEM((bm, d), F32),
                        pltpu.VMEM((d, de), BF16), pltpu.VMEM((d, de), BF16),
                        pltpu.VMEM((de, d), BF16),
                        pltpu.SemaphoreType.DMA((2,)), pltpu.SemaphoreType.DMA((1,))],
    )
    return pl.pallas_call(
        _expert_kernel,
        grid_spec=grid_spec,
        out_shape=jax.ShapeDtypeStruct((TOP_K, t + bm, d), F32),
        compiler_params=_cparams(("arbitrary",), 56),
        name="moe_experts",
    )(blk_expert, n_used, src, src, dst_k, dst_t, h, w_gate, w_up, w_down)


def _combine_kernel(x_ref, y0_ref, y1_ref, w_ref, mod_ref, o_ref):
    ff = w_ref[:, 0:1] * y0_ref[0] + w_ref[:, 1:2] * y1_ref[0]
    o_ref[...] = x_ref[...] + mod_ref[0, 5:6, :] * ff


def _combine(rows, x, y, wts, mod_l):
    t, d = x.shape
    tm = 256
    return pl.pallas_call(
        _combine_kernel,
        grid=(t // tm,),
        in_specs=[pl.BlockSpec((tm, d), lambda i: (i, 0)),
                  pl.BlockSpec((1, tm, d), lambda i: (0, i, 0)),
                  pl.BlockSpec((1, tm, d), lambda i: (1, i, 0)),
                  pl.BlockSpec((tm, TOP_K), lambda i: (i, 0)),
                  pl.BlockSpec((1, 6, d), lambda i: (rows.mod_row(i, tm), 0, 0))],
        out_specs=pl.BlockSpec((tm, d), lambda i: (i, 0)),
        out_shape=jax.ShapeDtypeStruct((t, d), F32),
        compiler_params=_cparams(("arbitrary",), 40),
        name="moe_combine",
    )(x, y, y, wts, mod_l)


def _moe(rows, x, mod_l, g, layer, w_group, b_group, w_route, b_route, w_gate, w_up, w_down):
    t, d = x.shape
    n_exp = w_route.shape[1]
    per_group = n_exp // N_GROUPS
    pad = LOGIT_LANES - N_GROUPS - n_exp
    w_r = jnp.concatenate([w_group, w_route, jnp.zeros((d, pad), F32)], axis=1)
    b_r = jnp.concatenate([b_group, b_route, jnp.zeros((pad,), F32)]).reshape(1, LOGIT_LANES)
    w_hi = w_r.astype(BF16)
    w_lo = (w_r - w_hi.astype(F32)).astype(BF16)
    h, logits = _ffn_norm_router(rows, x, mod_l, g, w_hi, w_lo, b_r)

    g_logit = logits[:, :N_GROUPS]
    g_idx = jnp.argmax(g_logit, axis=-1).astype(jnp.int32)
    g_w = 1.0 / jnp.sum(jnp.exp(g_logit - jnp.max(g_logit, axis=-1, keepdims=True)), axis=-1, keepdims=True)
    e_all = logits[:, N_GROUPS:N_GROUPS + n_exp].reshape(t, N_GROUPS, per_group)
    e_logit = jnp.take_along_axis(e_all, g_idx[:, None, None], axis=1)[:, 0]
    top_p, top_i = lax.top_k(jax.nn.softmax(e_logit, axis=-1), TOP_K)
    wts = g_w * top_p / jnp.sum(top_p, axis=-1, keepdims=True)
    eid = (g_idx[:, None] * per_group + top_i).reshape(-1).astype(jnp.int32)

    bm = MOE_ROWS
    n_assign = t * TOP_K
    order = jnp.argsort(eid).astype(jnp.int32)
    e_sorted = eid[order]
    bounds = jnp.searchsorted(e_sorted, jnp.arange(n_exp + 1, dtype=jnp.int32), side='left').astype(jnp.int32)
    start, sizes = bounds[:-1], bounds[1:] - bounds[:-1]
    padded = ((sizes + bm - 1) // bm) * bm
    pad_end = jnp.cumsum(padded)
    pad_start = pad_end - padded
    nblk = -(-(n_assign + n_exp * (bm - 1)) // bm)
    blk_start = jnp.arange(nblk, dtype=jnp.int32) * bm
    blk_expert = jnp.minimum(jnp.searchsorted(pad_end, blk_start, side='right'), n_exp - 1).astype(jnp.int32)
    n_used = (pad_end[-1:] // bm).astype(jnp.int32)
    blk_off = blk_start - pad_start[blk_expert]
    n_valid = jnp.clip(sizes[blk_expert] - blk_off, 0, bm)
    r = jnp.arange(bm, dtype=jnp.int32)[None, :]
    pos = jnp.clip(start[blk_expert][:, None] + blk_off[:, None] + r, 0, n_assign - 1)
    valid = r < n_valid[:, None]
    flat = order[pos]
    shape3 = (nblk, 1, bm)
    src = jnp.where(valid, flat // TOP_K, t - 1).astype(jnp.int32).reshape(shape3)
    dst_k = jnp.where(valid, flat % TOP_K, 0).astype(jnp.int32).reshape(shape3)
    dst_t = jnp.where(valid, flat // TOP_K, t + r).astype(jnp.int32).reshape(shape3)
    y = _experts(h, src, dst_k, dst_t, blk_expert, n_used, layer, w_gate, w_up, w_down)
    return _combine(rows, x, y, wts, mod_l)


def _rope_tables(s_lat):
    n_rows = s_lat // GRID_W
    row = jnp.repeat(jnp.arange(n_rows, dtype=F32), GRID_W)
    col = jnp.tile(jnp.arange(GRID_W, dtype=F32), n_rows)
    axis_dim = HEAD_DIM // 2
    inv_freq = ROPE_THETA ** (-jnp.arange(0, axis_dim, 2, dtype=F32) / axis_dim)
    ar, ac = row[:, None] * inv_freq, col[:, None] * inv_freq
    cos = jnp.concatenate([jnp.cos(ar), jnp.cos(ar), jnp.cos(ac), jnp.cos(ac)], axis=1)
    sin = jnp.concatenate([-jnp.sin(ar), jnp.sin(ar), -jnp.sin(ac), jnp.sin(ac)], axis=1)
    return cos, sin


def _cache_heads(c, transposed=False):
    perm = (0, 2, 3, 1) if transposed else (0, 2, 1, 3)
    return jnp.transpose(c, perm).astype(BF16)


def kernel(x_prompt, x_sample, cache_b_k, cache_b_v, cache_c_k, cache_c_v, c, c_ctx,
           w_mod, b_mod, norm_mix, norm_ffn, w_in_even, w_out_even, q_norm_b, k_norm_b, sink_b,
           w_in_odd, w_out_odd, q_norm_c, k_norm_c, w_group, b_group, w_route, b_route,
           w_gate, w_up, w_down):
    n_ctx_seq, s_ctx, d = x_prompt.shape
    n_lat_seq, s_lat, _ = x_sample.shape
    depth = w_mod.shape[0]
    rows = _Rows(n_ctx_seq, s_ctx, n_lat_seq, s_lat)
    assert 1 + n_lat_seq <= MOD_ROWS and rows.tc % ROW_TILE == 0 and s_lat % ROW_TILE == 0
    a_width = A_GROUPS * (d // 8)
    b_heads = (d - a_width) // HEAD_DIM
    b_kv = b_heads // GQA_GROUP
    c_heads = d // HEAD_DIM
    c_kv = c_heads // GQA_GROUP
    q_scale = HEAD_DIM ** -0.5

    cvec = jnp.concatenate([c_ctx[None, :], c, jnp.zeros((MOD_ROWS - 1 - n_lat_seq, d), F32)], axis=0)
    mod = _modulation(cvec, w_mod, b_mod).reshape(depth, MOD_ROWS, 6, d)

    x = jnp.concatenate([x_prompt.reshape(rows.tc, d), x_sample.reshape(rows.tl, d)], axis=0)
    rope = _rope_tables(s_lat)
    gd = a_width // A_GROUPS
    cc, sc = _dft_tables(gd)
    chan_cs = jnp.concatenate([cc, sc], axis=1)
    dft_ctx = _dft_tables(s_ctx)
    dft_lat = _dft_tables(s_lat)

    new_kv = []
    for l in range(depth):
        i = l // 2
        mod_l = mod[l]
        if l % 2 == 0:
            proj = _in_proj(rows, x, mod_l, norm_mix[l], w_in_even[i].astype(BF16))
            qc, kc, vc, kf, vf = _prep(proj, a_width, b_heads, b_kv, q_norm_b[i], k_norm_b[i],
                                       0, rows.tc, None, True, False, q_scale)
            q, k, v = _prep(proj, a_width, b_heads, b_kv, q_norm_b[i], k_norm_b[i],
                            rows.tc, rows.tl, rope, False, False, q_scale)
            sink = jnp.broadcast_to(sink_b[i][:, None, None], (b_heads, 1, HEAD_DIM))
            att = _ctx_attention(rows, qc, kc, vc, sink, b_heads * HEAD_DIM)
            att = _window_attention(rows, q, k, v, _cache_heads(cache_b_k[:, i]),
                                    _cache_heads(cache_b_v[:, i]), sink, att)
            uc, us = _chan_dft(proj, a_width, chan_cs)
            fm = _seq_dft(uc, us, dft_ctx[0], dft_ctx[1], 0, n_ctx_seq, None)
            fm = _seq_dft(uc, us, dft_lat[0], dft_lat[1], rows.tc, n_lat_seq, fm)
            x = _out_proj(rows, [fm, att], w_out_even[i].astype(BF16), x, mod_l)
            nkv = b_kv
        else:
            proj = _in_proj(rows, x, mod_l, norm_mix[l], w_in_odd[i].astype(BF16))
            qc, kc, vc, kf, vf = _prep(proj, 0, c_heads, c_kv, q_norm_c[i], k_norm_c[i],
                                       0, rows.tc, None, True, False, q_scale)
            qt, k, vt = _prep(proj, 0, c_heads, c_kv, q_norm_c[i], k_norm_c[i],
                              rows.tc, rows.tl, rope, False, True, q_scale * LOG2E)
            att = _ctx_attention(rows, qc, kc, vc, None, c_heads * HEAD_DIM)
            att = _dense_attention(rows, qt, k, vt, _cache_heads(cache_c_k[:, i]),
                                   _cache_heads(cache_c_v[:, i], True), att)
            x = _out_proj(rows, [att], w_out_odd[i].astype(BF16), x, mod_l)
            nkv = c_kv
        new_kv.append((kf.reshape(n_ctx_seq, s_ctx, nkv, HEAD_DIM),
                       vf.reshape(n_ctx_seq, s_ctx, nkv, HEAD_DIM)))
        x = _moe(rows, x, mod_l, norm_ffn[l], l, w_group[l], b_group[l], w_route[l], b_route[l],
                 w_gate, w_up, w_down)

    y_prompt = x[:rows.tc].reshape(n_ctx_seq, s_ctx, d)
    y_sample = x[rows.tc:].reshape(n_lat_seq, s_lat, d)
    new_b_k = jnp.stack([new_kv[l][0] for l in range(0, depth, 2)], axis=1)
    new_b_v = jnp.stack([new_kv[l][1] for l in range(0, depth, 2)], axis=1)
    new_c_k = jnp.stack([new_kv[l][0] for l in range(1, depth, 2)], axis=1)
    new_c_v = jnp.stack([new_kv[l][1] for l in range(1, depth, 2)], axis=1)
    return (y_prompt, y_sample, new_b_k, new_b_v, new_c_k, new_c_v)
```

```python
import functools
import math

import jax
import jax.numpy as jnp
from jax import lax
from jax.experimental import pallas as pl
from jax.experimental.pallas import tpu as pltpu

F32 = jnp.float32
BF16 = jnp.bfloat16

HEAD_DIM = 128
GRID_W = 64
ROPE_THETA = 10000.0
WINDOW = 128
RMS_EPS = 1e-6
NEG_INF = -1e30
A_GROUPS = 4
N_GROUPS = 8
TOP_K = 2
GQA_GROUP = 4
MOD_ROWS = 8
LOGIT_LANES = 128
ROW_TILE = 512
MOE_ROWS = 256
LOG2E = math.log2(math.e)
MIB = 1024 * 1024


def _cparams(sem, vmem_mib):
    return pltpu.CompilerParams(dimension_semantics=sem, vmem_limit_bytes=vmem_mib * MIB)


def _silu(x):
    return x / (1.0 + jnp.exp(-x))


def _dot(a, b):
    return jnp.dot(a, b, preferred_element_type=F32)


def _dot_t(a, b):
    return lax.dot_general(a, b, (((1,), (1,)), ((), ())), preferred_element_type=F32)


def _mod_kernel(c_ref, w_ref, b_ref, o_ref):
    s = _silu(c_ref[...]).astype(BF16)
    o_ref[0] = _dot(s, w_ref[0].astype(BF16)) + b_ref[0]


def _modulation(cvec, w_mod, b_mod):
    depth, d, n6 = w_mod.shape
    tn = 1024
    return pl.pallas_call(
        _mod_kernel,
        grid=(depth, n6 // tn),
        in_specs=[pl.BlockSpec((MOD_ROWS, d), lambda l, j: (0, 0)),
                  pl.BlockSpec((1, d, tn), lambda l, j: (l, 0, j)),
                  pl.BlockSpec((1, 1, tn), lambda l, j: (l, 0, j))],
        out_specs=pl.BlockSpec((1, MOD_ROWS, tn), lambda l, j: (l, 0, j)),
        out_shape=jax.ShapeDtypeStruct((depth, MOD_ROWS, n6), F32),
        compiler_params=_cparams(("arbitrary", "arbitrary"), 40),
        name="modulation",
    )(cvec, w_mod, b_mod.reshape(depth, 1, n6))


class _Rows:
    def __init__(self, n_ctx_seq, s_ctx, n_lat_seq, s_lat):
        self.n_ctx_seq, self.s_ctx, self.n_lat_seq, self.s_lat = n_ctx_seq, s_ctx, n_lat_seq, s_lat
        self.tc = n_ctx_seq * s_ctx
        self.tl = n_lat_seq * s_lat
        self.t = self.tc + self.tl

    def mod_row(self, i, tm):
        nct = self.tc // tm
        per = self.s_lat // tm
        return jnp.where(i < nct, 0, 1 + (i - nct) // per)


def _norm_mod(x, g, scale, shift):
    ms = jnp.mean(x * x, axis=-1, keepdims=True)
    y = x * lax.rsqrt(ms + RMS_EPS) * g
    return y * (1.0 + scale) + shift


def _in_proj_kernel(x_ref, mod_ref, g_ref, w_ref, o_ref, h_ref):
    i = pl.program_id(0)
    slot = i % 2

    @pl.when(i == 0)
    def _():
        h_ref[1] = jnp.zeros(h_ref.shape[1:], BF16)

    h = _norm_mod(x_ref[...], g_ref[...], mod_ref[0, 1:2, :], mod_ref[0, 0:1, :])
    o_ref[...] = _dot(h_ref[1 - slot], w_ref[...])
    h_ref[slot] = h.astype(BF16)


def _in_proj(rows, x, mod_l, g, w_bf16):
    t, d = x.shape
    n = w_bf16.shape[1]
    tm = ROW_TILE
    nt = t // tm

    def cur(i):
        return jnp.minimum(i, nt - 1)

    return pl.pallas_call(
        _in_proj_kernel,
        grid=(nt + 1,),
        in_specs=[pl.BlockSpec((tm, d), lambda i: (cur(i), 0)),
                  pl.BlockSpec((1, 6, d), lambda i: (rows.mod_row(cur(i), tm), 0, 0)),
                  pl.BlockSpec((1, d), lambda i: (0, 0)),
                  pl.BlockSpec((d, n), lambda i: (0, 0))],
        out_specs=pl.BlockSpec((tm, n), lambda i: (jnp.maximum(i - 1, 0), 0)),
        out_shape=jax.ShapeDtypeStruct((t, n), F32),
        scratch_shapes=[pltpu.VMEM((2, tm, d), BF16)],
        compiler_params=_cparams(("arbitrary",), 56),
        name="in_proj",
    )(x, mod_l, g.reshape(1, d), w_bf16)


def _swap32(y):
    lane = lax.broadcasted_iota(jnp.int32, y.shape, 1)
    return jnp.where((lane & 63) < 32, pltpu.roll(y, 96, 1), pltpu.roll(y, 32, 1))


def _prep_kernel(*refs, nq, nkv, rope, emit_f32, transposed, q_scale):
    q_ref, k_ref, v_ref, qn_ref, kn_ref = refs[:5]
    pos = 5
    if rope:
        cos_ref, sin_ref = refs[pos:pos + 2]
        pos += 2
    qo_ref, ko_ref, vo_ref = refs[pos:pos + 3]
    pos += 3
    if emit_f32:
        kf_ref, vf_ref = refs[pos:pos + 2]

    def head_norm(x, gain):
        ms = jnp.mean(x * x, axis=-1, keepdims=True)
        return x * lax.rsqrt(ms + RMS_EPS) * gain

    def rot(y):
        if not rope:
            return y
        return y * cos_ref[...] + _swap32(y) * sin_ref[...]

    for h in range(nq):
        sl = slice(h * HEAD_DIM, (h + 1) * HEAD_DIM)
        y = rot(head_norm(q_ref[:, sl], qn_ref[...])) * q_scale
        qo_ref[h] = y.astype(BF16)
    for h in range(nkv):
        sl = slice(h * HEAD_DIM, (h + 1) * HEAD_DIM)
        y = head_norm(k_ref[:, sl], kn_ref[...])
        v = v_ref[:, sl]
        if emit_f32:
            kf_ref[:, sl] = y
            vf_ref[:, sl] = v
        ko_ref[h] = rot(y).astype(BF16)
        vo_ref[h] = (v.T if transposed else v).astype(BF16)


def _prep(proj, col0, nq, nkv, qn, kn, row0, n_rows, rope_tabs, emit_f32, transposed, q_scale):
    tm = 256
    rope = rope_tabs is not None
    r0 = row0 // tm
    qw, kw = nq * HEAD_DIM, nkv * HEAD_DIM
    qb, kb, vb = col0 // qw, (col0 + qw) // kw, (col0 + qw + kw) // kw
    assert col0 % qw == 0 and (col0 + qw) % kw == 0 and row0 % tm == 0
    in_specs = [pl.BlockSpec((tm, qw), lambda i: (r0 + i, qb)),
                pl.BlockSpec((tm, kw), lambda i: (r0 + i, kb)),
                pl.BlockSpec((tm, kw), lambda i: (r0 + i, vb)),
                pl.BlockSpec((1, HEAD_DIM), lambda i: (0, 0)),
                pl.BlockSpec((1, HEAD_DIM), lambda i: (0, 0))]
    args = [proj, proj, proj, qn.reshape(1, HEAD_DIM), kn.reshape(1, HEAD_DIM)]
    if rope:
        per = rope_tabs[0].shape[0] // tm
        in_specs += [pl.BlockSpec((tm, HEAD_DIM), lambda i: (i % per, 0)),
                     pl.BlockSpec((tm, HEAD_DIM), lambda i: (i % per, 0))]
        args += list(rope_tabs)
    row_major = pl.BlockSpec((nq, tm, HEAD_DIM), lambda i: (0, i, 0))
    kv_row = pl.BlockSpec((nkv, tm, HEAD_DIM), lambda i: (0, i, 0))
    kv_col = pl.BlockSpec((nkv, HEAD_DIM, tm), lambda i: (0, 0, i))
    if transposed:
        out_specs = [row_major, kv_row, kv_col]
        out_shape = [jax.ShapeDtypeStruct((nq, n_rows, HEAD_DIM), BF16),
                     jax.ShapeDtypeStruct((nkv, n_rows, HEAD_DIM), BF16),
                     jax.ShapeDtypeStruct((nkv, HEAD_DIM, n_rows), BF16)]
    else:
        out_specs = [row_major, kv_row, kv_row]
        out_shape = [jax.ShapeDtypeStruct((nq, n_rows, HEAD_DIM), BF16),
                     jax.ShapeDtypeStruct((nkv, n_rows, HEAD_DIM), BF16),
                     jax.ShapeDtypeStruct((nkv, n_rows, HEAD_DIM), BF16)]
    if emit_f32:
        out_specs += [pl.BlockSpec((tm, kw), lambda i: (i, 0)),
                      pl.BlockSpec((tm, kw), lambda i: (i, 0))]
        out_shape += [jax.ShapeDtypeStruct((n_rows, kw), F32),
                      jax.ShapeDtypeStruct((n_rows, kw), F32)]
    return pl.pallas_call(
        functools.partial(_prep_kernel, nq=nq, nkv=nkv, rope=rope, emit_f32=emit_f32,
                          transposed=transposed, q_scale=q_scale),
        grid=(n_rows // tm,),
        in_specs=in_specs,
        out_specs=out_specs,
        out_shape=out_shape,
        compiler_params=_cparams(("arbitrary",), 40),
        name="qkv_prep_rope" if rope else "qkv_prep",
    )(*args)


def _softmax_pv(s, v_list, widths, sink):
    m = jnp.max(s, axis=-1, keepdims=True)
    if sink is not None:
        m = jnp.maximum(m, sink)
    p = jnp.exp(s - m)
    l = jnp.sum(p, axis=-1, keepdims=True)
    if sink is not None:
        l = l + jnp.exp(sink - m)
    pb = p.astype(BF16)
    o = None
    c0 = 0
    for v, w in zip(v_list, widths):
        part = _dot(pb[:, c0:c0 + w], v)
        o = part if o is None else o + part
        c0 += w
    return o / l


def _ctx_attn_kernel(*refs, has_sink):
    if has_sink:
        q_ref, k_ref, v_ref, sink_ref, o_ref = refs
    else:
        q_ref, k_ref, v_ref, o_ref = refs
    k = k_ref[0]
    v = v_ref[0]
    for g in range(GQA_GROUP):
        s = _dot_t(q_ref[g], k)
        sink = sink_ref[g][:, 0:1] if has_sink else None
        o = _softmax_pv(s, [v], [k.shape[0]], sink)
        o_ref[:, g * HEAD_DIM:(g + 1) * HEAD_DIM] = o.astype(BF16)


def _ctx_attention(rows, q, k, v, sink, att_cols):
    nkv = k.shape[0]
    s = rows.s_ctx
    gw = GQA_GROUP * HEAD_DIM
    has_sink = sink is not None
    in_specs = [pl.BlockSpec((GQA_GROUP, s, HEAD_DIM), lambda b, h: (h, b, 0)),
                pl.BlockSpec((1, s, HEAD_DIM), lambda b, h: (h, b, 0)),
                pl.BlockSpec((1, s, HEAD_DIM), lambda b, h: (h, b, 0))]
    args = [q, k, v]
    if has_sink:
        in_specs.append(pl.BlockSpec((GQA_GROUP, 1, HEAD_DIM), lambda b, h: (h, 0, 0)))
        args.append(sink)
    return pl.pallas_call(
        functools.partial(_ctx_attn_kernel, has_sink=has_sink),
        grid=(rows.n_ctx_seq, nkv),
        in_specs=in_specs,
        out_specs=pl.BlockSpec((s, gw), lambda b, h: (b, h)),
        out_shape=jax.ShapeDtypeStruct((rows.t, att_cols), BF16),
        compiler_params=_cparams(("arbitrary", "arbitrary"), 40),
        name="ctx_attention",
    )(*args)


def _window_attn_kernel(q_ref, kp_ref, kc_ref, kn_ref, vp_ref, vc_ref, vn_ref, kx_ref, vx_ref,
                        sink_ref, prev_ref, o_ref, *, nkv, n_blocks):
    del prev_ref
    n = pl.program_id(1)
    qb = WINDOW
    m_rows = GQA_GROUP * qb
    r = lax.broadcasted_iota(jnp.int32, (m_rows, qb), 0) & (qb - 1)
    c = lax.broadcasted_iota(jnp.int32, (m_rows, qb), 1)
    ok_prev = (c >= r) & (n > 0)
    ok_next = (c <= r) & (n < n_blocks - 1)
    n_ctx = kx_ref.shape[2]
    for h in range(nkv):
        q = q_ref[h * GQA_GROUP:(h + 1) * GQA_GROUP].reshape(m_rows, HEAD_DIM)
        s = jnp.concatenate(
            [jnp.where(ok_prev, _dot_t(q, kp_ref[h]), NEG_INF),
             _dot_t(q, kc_ref[h]),
             jnp.where(ok_next, _dot_t(q, kn_ref[h]), NEG_INF),
             _dot_t(q, kx_ref[0, h])], axis=-1)
        sink = jnp.concatenate(
            [jnp.broadcast_to(sink_ref[h * GQA_GROUP + g][:, 0:1], (qb, 1)) for g in range(GQA_GROUP)],
            axis=0)
        o = _softmax_pv(s, [vp_ref[h], vc_ref[h], vn_ref[h], vx_ref[0, h]], [qb, qb, qb, n_ctx], sink)
        for g in range(GQA_GROUP):
            col = (h * GQA_GROUP + g) * HEAD_DIM
            o_ref[:, col:col + HEAD_DIM] = o[g * qb:(g + 1) * qb].astype(BF16)


def _window_attention(rows, q, k, v, kx, vx, sink, prev):
    nq, nkv = q.shape[0], k.shape[0]
    qb = WINDOW
    nb = rows.s_lat // qb
    out0 = rows.tc // qb
    n_ctx = kx.shape[2]

    def kv_spec(shift):
        def imap(b, n):
            return (0, b * nb + jnp.clip(n + shift, 0, nb - 1), 0)
        return pl.BlockSpec((nkv, qb, HEAD_DIM), imap)

    cache_spec = pl.BlockSpec((1, nkv, n_ctx, HEAD_DIM), lambda b, n: (b, 0, 0, 0))
    return pl.pallas_call(
        functools.partial(_window_attn_kernel, nkv=nkv, n_blocks=nb),
        grid=(rows.n_lat_seq, nb),
        in_specs=[pl.BlockSpec((nq, qb, HEAD_DIM), lambda b, n: (0, b * nb + n, 0)),
                  kv_spec(-1), kv_spec(0), kv_spec(1),
                  kv_spec(-1), kv_spec(0), kv_spec(1),
                  cache_spec, cache_spec,
                  pl.BlockSpec((nq, 1, HEAD_DIM), lambda b, n: (0, 0, 0)),
                  pl.BlockSpec(memory_space=pl.ANY)],
        out_specs=pl.BlockSpec((qb, nq * HEAD_DIM), lambda b, n: (out0 + b * nb + n, 0)),
        out_shape=jax.ShapeDtypeStruct(prev.shape, BF16),
        input_output_aliases={10: 0},
        compiler_params=_cparams(("arbitrary", "arbitrary"), 40),
        name="window_attention",
    )(q, k, k, k, v, v, v, kx, vx, sink, prev)


def _dense_attn_kernel(q_ref, k_ref, vt_ref, kx_ref, vxt_ref, prev_ref, o_ref, *, bq, bk):
    del prev_ref
    m_cols = GQA_GROUP * bq
    q = q_ref[...].reshape(m_cols, HEAD_DIM)

    def update(k, vt, carry):
        m_old, l_old, acc = carry
        s = _dot_t(k, q)
        m_new = jnp.maximum(m_old, jnp.max(s, axis=0, keepdims=True))
        alpha = jnp.exp2(m_old - m_new)
        p = jnp.exp2(s - m_new)
        l_new = alpha * l_old + jnp.sum(p, axis=0, keepdims=True)
        acc = alpha * acc + _dot(vt, p.astype(BF16))
        return m_new, l_new, acc

    carry = (jnp.full((1, m_cols), NEG_INF, F32), jnp.zeros((1, m_cols), F32),
             jnp.zeros((HEAD_DIM, m_cols), F32))
    for c in range(k_ref.shape[1] // bk):
        carry = update(k_ref[0, c * bk:(c + 1) * bk, :], vt_ref[0, :, c * bk:(c + 1) * bk], carry)
    _, l, acc = update(kx_ref[0, 0], vxt_ref[0, 0], carry)
    o = acc / l
    for g in range(GQA_GROUP):
        o_ref[:, g * HEAD_DIM:(g + 1) * HEAD_DIM] = o[:, g * bq:(g + 1) * bq].T.astype(BF16)


def _dense_attention(rows, q, k, vt, kx, vxt, prev):
    nkv = k.shape[0]
    bq, bk = 256, 512
    s = rows.s_lat
    assert s % bk == 0 and s % bq == 0
    nqb = s // bq
    out0 = rows.tc // bq
    n_ctx = kx.shape[2]
    return pl.pallas_call(
        functools.partial(_dense_attn_kernel, bq=bq, bk=bk),
        grid=(rows.n_lat_seq, nkv, nqb),
        in_specs=[pl.BlockSpec((GQA_GROUP, bq, HEAD_DIM), lambda b, h, i: (h, b * nqb + i, 0)),
                  pl.BlockSpec((1, s, HEAD_DIM), lambda b, h, i: (h, b, 0)),
                  pl.BlockSpec((1, HEAD_DIM, s), lambda b, h, i: (h, 0, b)),
                  pl.BlockSpec((1, 1, n_ctx, HEAD_DIM), lambda b, h, i: (b, h, 0, 0)),
                  pl.BlockSpec((1, 1, HEAD_DIM, n_ctx), lambda b, h, i: (b, h, 0, 0)),
                  pl.BlockSpec(memory_space=pl.ANY)],
        out_specs=pl.BlockSpec((bq, GQA_GROUP * HEAD_DIM), lambda b, h, i: (out0 + b * nqb + i, h)),
        out_shape=jax.ShapeDtypeStruct(prev.shape, BF16),
        input_output_aliases={5: 0},
        compiler_params=_cparams(("arbitrary", "arbitrary", "arbitrary"), 48),
        name="dense_attention",
    )(q, k, vt, kx, vxt, prev)


def _dft_tables(n):
    j = jnp.arange(n, dtype=jnp.int32)
    jk = (j[:, None] * j[None, :]) % n
    ang = jk.astype(F32) * (2.0 * math.pi / n)
    scale = n ** -0.5
    return (jnp.cos(ang) * scale).astype(BF16), (jnp.sin(ang) * scale).astype(BF16)


def _chan_dft_kernel(a_ref, cs_ref, uc_ref, us_ref, *, gd):
    for g in range(A_GROUPS):
        sl = slice(g * gd, (g + 1) * gd)
        y = _dot(a_ref[:, sl].astype(BF16), cs_ref[...])
        uc_ref[:, sl] = y[:, :gd].astype(BF16)
        us_ref[:, sl] = y[:, gd:].astype(BF16)


def _chan_dft(proj, a_width, cs):
    t = proj.shape[0]
    gd = a_width // A_GROUPS
    tm = ROW_TILE
    return pl.pallas_call(
        functools.partial(_chan_dft_kernel, gd=gd),
        grid=(t // tm,),
        in_specs=[pl.BlockSpec((tm, a_width), lambda i: (i, 0)),
                  pl.BlockSpec((gd, 2 * gd), lambda i: (0, 0))],
        out_specs=[pl.BlockSpec((tm, a_width), lambda i: (i, 0)),
                   pl.BlockSpec((tm, a_width), lambda i: (i, 0))],
        out_shape=[jax.ShapeDtypeStruct((t, a_width), BF16)] * 2,
        compiler_params=_cparams(("arbitrary",), 40),
        name="channel_dft",
    )(proj, cs)


def _seq_dft_kernel(c_ref, s_ref, uc_ref, us_ref, *rest):
    o_ref = rest[-1]
    o_ref[...] = (_dot(c_ref[...], uc_ref[...]) - _dot(s_ref[...], us_ref[...])).astype(BF16)


def _seq_dft(uc, us, cmat, smat, row0, n_seq, prev):
    t, width = uc.shape
    s = cmat.shape[0]
    tm = min(s, ROW_TILE)
    tn = 512
    ni = s // tm
    seq0 = row0 // s
    out0 = row0 // tm
    in_specs = [pl.BlockSpec((tm, s), lambda b, j, i: (i, 0)),
                pl.BlockSpec((tm, s), lambda b, j, i: (i, 0)),
                pl.BlockSpec((s, tn), lambda b, j, i: (seq0 + b, j)),
                pl.BlockSpec((s, tn), lambda b, j, i: (seq0 + b, j))]
    args = [cmat, smat, uc, us]
    aliases = {}
    if prev is not None:
        in_specs.append(pl.BlockSpec(memory_space=pl.ANY))
        args.append(prev)
        aliases = {4: 0}
    return pl.pallas_call(
        _seq_dft_kernel,
        grid=(n_seq, width // tn, ni),
        in_specs=in_specs,
        out_specs=pl.BlockSpec((tm, tn), lambda b, j, i: (out0 + b * ni + i, j)),
        out_shape=jax.ShapeDtypeStruct((t, width), BF16),
        input_output_aliases=aliases,
        compiler_params=_cparams(("arbitrary", "arbitrary", "arbitrary"), 48),
        name="sequence_dft",
    )(*args)


def _out_proj_kernel(*refs, n_lhs):
    lhs = refs[:n_lhs]
    ws = refs[n_lhs:2 * n_lhs]
    x_ref, mod_ref, o_ref = refs[2 * n_lhs:]
    acc = None
    for a, w in zip(lhs, ws):
        part = _dot(a[...], w[...])
        acc = part if acc is None else acc + part
    o_ref[...] = x_ref[...] + mod_ref[0, 2:3, :] * acc


def _out_proj(rows, lhs_list, w_bf16, x, mod_l):
    t, d = x.shape
    tm = ROW_TILE
    in_specs, args, w_args, w_specs = [], [], [], []
    k0 = 0
    for a in lhs_list:
        kw = a.shape[1]
        in_specs.append(pl.BlockSpec((tm, kw), lambda i: (i, 0)))
        args.append(a)
        kb = k0 // kw
        assert k0 % kw == 0
        w_specs.append(pl.BlockSpec((kw, d), lambda i, kb=kb: (kb, 0)))
        w_args.append(w_bf16)
        k0 += kw
    in_specs += w_specs + [pl.BlockSpec((tm, d), lambda i: (i, 0)),
                           pl.BlockSpec((1, 6, d), lambda i: (rows.mod_row(i, tm), 0, 0))]
    args += w_args + [x, mod_l]
    return pl.pallas_call(
        functools.partial(_out_proj_kernel, n_lhs=len(lhs_list)),
        grid=(t // tm,),
        in_specs=in_specs,
        out_specs=pl.BlockSpec((tm, d), lambda i: (i, 0)),
        out_shape=jax.ShapeDtypeStruct((t, d), F32),
        compiler_params=_cparams(("arbitrary",), 56),
        name="out_proj",
    )(*args)


def _ffn_norm_kernel(x_ref, mod_ref, g_ref, whi_ref, wlo_ref, b_ref, h_ref, lg_ref):
    h = _norm_mod(x_ref[...], g_ref[...], mod_ref[0, 4:5, :], mod_ref[0, 3:4, :])
    hi = h.astype(BF16)
    lo = (h - hi.astype(F32)).astype(BF16)
    h_ref[...] = h
    lg_ref[...] = (_dot(hi, whi_ref[...]) + _dot(hi, wlo_ref[...]) + _dot(lo, whi_ref[...])
                   + b_ref[...])


def _ffn_norm_router(rows, x, mod_l, g, w_hi, w_lo, b_r):
    t, d = x.shape
    tm = ROW_TILE
    return pl.pallas_call(
        _ffn_norm_kernel,
        grid=(t // tm,),
        in_specs=[pl.BlockSpec((tm, d), lambda i: (i, 0)),
                  pl.BlockSpec((1, 6, d), lambda i: (rows.mod_row(i, tm), 0, 0)),
                  pl.BlockSpec((1, d), lambda i: (0, 0)),
                  pl.BlockSpec((d, LOGIT_LANES), lambda i: (0, 0)),
                  pl.BlockSpec((d, LOGIT_LANES), lambda i: (0, 0)),
                  pl.BlockSpec((1, LOGIT_LANES), lambda i: (0, 0))],
        out_specs=[pl.BlockSpec((tm, d), lambda i: (i, 0)),
                   pl.BlockSpec((tm, LOGIT_LANES), lambda i: (i, 0))],
        out_shape=[jax.ShapeDtypeStruct((t, d), F32),
                   jax.ShapeDtypeStruct((t, LOGIT_LANES), F32)],
        compiler_params=_cparams(("arbitrary",), 40),
        name="ffn_norm_router",
    )(x, mod_l, g.reshape(1, d), w_hi, w_lo, b_r)


def _dispatch_kernel(nb_ref, src_ref, nsrc_ref, h_hbm, o_ref, xbuf, gsem):
    b = pl.program_id(0)
    n_used = nb_ref[0]
    bm = xbuf.shape[1]
    slot = b % 2

    def gather_copy(ref, r, dst_slot):
        return pltpu.make_async_copy(h_hbm.at[pl.ds(ref[0, 0, r], 1)], xbuf.at[dst_slot, pl.ds(r, 1)],
                                     gsem.at[dst_slot])

    @pl.when(b == 0)
    def _():
        for r in range(bm):
            gather_copy(src_ref, r, 0).start(priority=r % 2)

    @pl.when(b < n_used)
    def _():
        for r in range(bm):
            gather_copy(src_ref, r, slot).wait()
        for r in range(bm):
            gather_copy(nsrc_ref, r, 1 - slot).start(priority=r % 2)
        o_ref[...] = xbuf[slot].astype(BF16)

        @pl.when(b == n_used - 1)
        def _():
            for r in range(bm):
                gather_copy(nsrc_ref, r, 1 - slot).wait()

    @pl.when(b >= n_used)
    def _():
        o_ref[...] = jnp.zeros(o_ref.shape, BF16)


def _dispatch(h, src, n_used):
    t, d = h.shape
    bm = MOE_ROWS
    nblk = src.shape[0]

    def cur(b, nb):
        return (b, 0, 0)

    def nxt(b, nb):
        return (jnp.minimum(b + 1, jnp.maximum(nb[0] - 1, 0)), 0, 0)

    def smem(imap):
        return pl.BlockSpec((1, 1, bm), imap, memory_space=pltpu.SMEM)

    grid_spec = pltpu.PrefetchScalarGridSpec(
        num_scalar_prefetch=1,
        grid=(nblk,),
        in_specs=[smem(cur), smem(nxt), pl.BlockSpec(memory_space=pl.ANY)],
        out_specs=pl.BlockSpec((bm, d), lambda b, nb: (b, 0)),
        scratch_shapes=[pltpu.VMEM((2, bm, d), F32), pltpu.SemaphoreType.DMA((2,))],
    )
    return pl.pallas_call(
        _dispatch_kernel,
        grid_spec=grid_spec,
        out_shape=jax.ShapeDtypeStruct((nblk * bm, d), BF16),
        compiler_params=_cparams(("arbitrary",), 40),
        name="moe_dispatch",
    )(n_used, src, src, h)


def _expert_kernel(be_ref, nb_ref, x_ref, wg_ref, wu_ref, wd_ref, o_ref, wgb, wub, wdb):
    b = pl.program_id(0)
    n_used = nb_ref[0]

    @pl.when(b < n_used)
    def _():
        @pl.when((b == 0) | (be_ref[b] != be_ref[jnp.maximum(b - 1, 0)]))
        def _():
            wgb[...] = wg_ref[0, 0].astype(BF16)
            wub[...] = wu_ref[0, 0].astype(BF16)
            wdb[...] = wd_ref[0, 0].astype(BF16)

        x = x_ref[...]
        hid = (_silu(_dot(x, wgb[...])) * _dot(x, wub[...])).astype(BF16)
        o_ref[...] = _dot(hid, wdb[...])

    @pl.when(b >= n_used)
    def _():
        o_ref[...] = jnp.zeros(o_ref.shape, F32)


def _experts(buf, blk_expert, n_used, layer, w_gate, w_up, w_down):
    nrows, d = buf.shape
    de = w_gate.shape[3]
    bm = MOE_ROWS
    nblk = nrows // bm

    def w_map(b, be, nb):
        return (layer, be[b], 0, 0)

    def x_map(b, be, nb):
        return (jnp.minimum(b, jnp.maximum(nb[0] - 1, 0)), 0)

    grid_spec = pltpu.PrefetchScalarGridSpec(
        num_scalar_prefetch=2,
        grid=(nblk,),
        in_specs=[pl.BlockSpec((bm, d), x_map),
                  pl.BlockSpec((1, 1, d, de), w_map),
                  pl.BlockSpec((1, 1, d, de), w_map),
                  pl.BlockSpec((1, 1, de, d), w_map)],
        out_specs=pl.BlockSpec((bm, d), lambda b, be, nb: (b, 0)),
        scratch_shapes=[pltpu.VMEM((d, de), BF16), pltpu.VMEM((d, de), BF16),
                        pltpu.VMEM((de, d), BF16)],
    )
    return pl.pallas_call(
        _expert_kernel,
        grid_spec=grid_spec,
        out_shape=jax.ShapeDtypeStruct((nrows, d), F32),
        compiler_params=_cparams(("arbitrary",), 56),
        name="moe_experts",
    )(blk_expert, n_used, buf, w_gate, w_up, w_down)


def _combine_kernel(pos_ref, npos_ref, x_ref, ys_hbm, w_ref, mod_ref, o_ref, ybuf, sem):
    i = pl.program_id(0)
    slot = i % 2
    tm = x_ref.shape[0]

    def gather_copy(ref, j, dst_slot):
        return pltpu.make_async_copy(ys_hbm.at[pl.ds(ref[0, 0, j], 1)],
                                     ybuf.at[dst_slot, j % TOP_K, pl.ds(j // TOP_K, 1)], sem.at[dst_slot])

    @pl.when(i == 0)
    def _():
        for j in range(TOP_K * tm):
            gather_copy(pos_ref, j, 0).start(priority=j % 2)

    for j in range(TOP_K * tm):
        gather_copy(pos_ref, j, slot).wait()
    for j in range(TOP_K * tm):
        gather_copy(npos_ref, j, 1 - slot).start(priority=j % 2)
    ff = w_ref[:, 0:1] * ybuf[slot, 0] + w_ref[:, 1:2] * ybuf[slot, 1]
    o_ref[...] = x_ref[...] + mod_ref[0, 5:6, :] * ff

    @pl.when(i == pl.num_programs(0) - 1)
    def _():
        for j in range(TOP_K * tm):
            gather_copy(npos_ref, j, 1 - slot).wait()


def _combine(rows, x, ys, pos, wts, mod_l):
    t, d = x.shape
    nt, _, ptm = pos.shape
    tm = ptm // TOP_K

    def smem(imap):
        return pl.BlockSpec((1, 1, ptm), imap, memory_space=pltpu.SMEM)

    return pl.pallas_call(
        _combine_kernel,
        grid=(nt,),
        in_specs=[smem(lambda i: (i, 0, 0)),
                  smem(lambda i: (jnp.minimum(i + 1, nt - 1), 0, 0)),
                  pl.BlockSpec((tm, d), lambda i: (i, 0)),
                  pl.BlockSpec(memory_space=pl.ANY),
                  pl.BlockSpec((tm, TOP_K), lambda i: (i, 0)),
                  pl.BlockSpec((1, 6, d), lambda i: (rows.mod_row(i, tm), 0, 0))],
        out_specs=pl.BlockSpec((tm, d), lambda i: (i, 0)),
        out_shape=jax.ShapeDtypeStruct((t, d), F32),
        scratch_shapes=[pltpu.VMEM((2, TOP_K, tm, d), F32), pltpu.SemaphoreType.DMA((2,))],
        compiler_params=_cparams(("arbitrary",), 40),
        name="moe_combine",
    )(pos, pos, x, ys, wts, mod_l)


def _moe(rows, x, mod_l, g, layer, w_group, b_group, w_route, b_route, w_gate, w_up, w_down):
    t, d = x.shape
    n_exp = w_route.shape[1]
    per_group = n_exp // N_GROUPS
    pad = LOGIT_LANES - N_GROUPS - n_exp
    w_r = jnp.concatenate([w_group, w_route, jnp.zeros((d, pad), F32)], axis=1)
    b_r = jnp.concatenate([b_group, b_route, jnp.zeros((pad,), F32)]).reshape(1, LOGIT_LANES)
    w_hi = w_r.astype(BF16)
    w_lo = (w_r - w_hi.astype(F32)).astype(BF16)
    h, logits = _ffn_norm_router(rows, x, mod_l, g, w_hi, w_lo, b_r)

    g_logit = logits[:, :N_GROUPS]
    g_idx = jnp.argmax(g_logit, axis=-1).astype(jnp.int32)
    g_w = 1.0 / jnp.sum(jnp.exp(g_logit - jnp.max(g_logit, axis=-1, keepdims=True)), axis=-1, keepdims=True)
    e_all = logits[:, N_GROUPS:N_GROUPS + n_exp].reshape(t, N_GROUPS, per_group)
    e_logit = jnp.take_along_axis(e_all, g_idx[:, None, None], axis=1)[:, 0]
    prob = jax.nn.softmax(e_logit, axis=-1)
    lane = lax.broadcasted_iota(jnp.int32, prob.shape, 1)
    i1 = jnp.argmax(prob, axis=-1).astype(jnp.int32)[:, None]
    p1 = jnp.max(prob, axis=-1, keepdims=True)
    rest = jnp.where(lane == i1, -1.0, prob)
    i2 = jnp.argmax(rest, axis=-1).astype(jnp.int32)[:, None]
    p2 = jnp.max(rest, axis=-1, keepdims=True)
    top_p = jnp.concatenate([p1, p2], axis=1)
    top_i = jnp.concatenate([i1, i2], axis=1)
    wts = g_w * top_p / jnp.sum(top_p, axis=-1, keepdims=True)
    eid = (g_idx[:, None] * per_group + top_i).reshape(-1).astype(jnp.int32)

    bm = MOE_ROWS
    n_assign = t * TOP_K
    order = jnp.argsort(eid).astype(jnp.int32)
    e_sorted = eid[order]
    experts = jnp.arange(n_exp + 1, dtype=jnp.int32)
    bounds = jnp.sum((e_sorted[None, :] < experts[:, None]).astype(jnp.int32), axis=1)
    start, sizes = bounds[:-1], bounds[1:] - bounds[:-1]
    padded = ((sizes + bm - 1) // bm) * bm
    pad_end = jnp.cumsum(padded)
    pad_start = pad_end - padded
    nblk = -(-(n_assign + n_exp * (bm - 1)) // bm)
    blk_start = jnp.arange(nblk, dtype=jnp.int32) * bm
    blk_expert = jnp.minimum(jnp.sum((pad_end[None, :] <= blk_start[:, None]).astype(jnp.int32), axis=1),
                             n_exp - 1).astype(jnp.int32)
    n_used = (pad_end[-1:] // bm).astype(jnp.int32)
    blk_off = blk_start - pad_start[blk_expert]
    n_valid = jnp.clip(sizes[blk_expert] - blk_off, 0, bm)
    r = jnp.arange(bm, dtype=jnp.int32)[None, :]
    pos = jnp.clip(start[blk_expert][:, None] + blk_off[:, None] + r, 0, n_assign - 1)
    valid = r < n_valid[:, None]
    flat = order[pos]
    src = jnp.where(valid, flat // TOP_K, t - 1).astype(jnp.int32).reshape(nblk, 1, bm)
    buf = _dispatch(h, src, n_used)
    ys = _experts(buf, blk_expert, n_used, layer, w_gate, w_up, w_down)
    rank = jnp.argsort(order).astype(jnp.int32)
    where = pad_start[eid] + rank - start[eid]
    tm = 256
    return _combine(rows, x, ys, where.reshape(t // tm, 1, TOP_K * tm), wts, mod_l)


def _rope_tables(s_lat):
    n_rows = s_lat // GRID_W
    row = jnp.repeat(jnp.arange(n_rows, dtype=F32), GRID_W)
    col = jnp.tile(jnp.arange(GRID_W, dtype=F32), n_rows)
    axis_dim = HEAD_DIM // 2
    inv_freq = ROPE_THETA ** (-jnp.arange(0, axis_dim, 2, dtype=F32) / axis_dim)
    ar, ac = row[:, None] * inv_freq, col[:, None] * inv_freq
    cos = jnp.concatenate([jnp.cos(ar), jnp.cos(ar), jnp.cos(ac), jnp.cos(ac)], axis=1)
    sin = jnp.concatenate([-jnp.sin(ar), jnp.sin(ar), -jnp.sin(ac), jnp.sin(ac)], axis=1)
    return cos, sin


def _cache_heads(c, transposed=False):
    perm = (0, 2, 3, 1) if transposed else (0, 2, 1, 3)
    return jnp.transpose(c, perm).astype(BF16)


def kernel(x_prompt, x_sample, cache_b_k, cache_b_v, cache_c_k, cache_c_v, c, c_ctx,
           w_mod, b_mod, norm_mix, norm_ffn, w_in_even, w_out_even, q_norm_b, k_norm_b, sink_b,
           w_in_odd, w_out_odd, q_norm_c, k_norm_c, w_group, b_group, w_route, b_route,
           w_gate, w_up, w_down):
    n_ctx_seq, s_ctx, d = x_prompt.shape
    n_lat_seq, s_lat, _ = x_sample.shape
    depth = w_mod.shape[0]
    rows = _Rows(n_ctx_seq, s_ctx, n_lat_seq, s_lat)
    assert 1 + n_lat_seq <= MOD_ROWS and rows.tc % ROW_TILE == 0 and s_lat % ROW_TILE == 0
    a_width = A_GROUPS * (d // 8)
    b_heads = (d - a_width) // HEAD_DIM
    b_kv = b_heads // GQA_GROUP
    c_heads = d // HEAD_DIM
    c_kv = c_heads // GQA_GROUP
    q_scale = HEAD_DIM ** -0.5

    cvec = jnp.concatenate([c_ctx[None, :], c, jnp.zeros((MOD_ROWS - 1 - n_lat_seq, d), F32)], axis=0)
    mod = _modulation(cvec, w_mod, b_mod).reshape(depth, MOD_ROWS, 6, d)

    x = jnp.concatenate([x_prompt.reshape(rows.tc, d), x_sample.reshape(rows.tl, d)], axis=0)
    rope = _rope_tables(s_lat)
    gd = a_width // A_GROUPS
    cc, sc = _dft_tables(gd)
    chan_cs = jnp.concatenate([cc, sc], axis=1)
    dft_ctx = _dft_tables(s_ctx)
    dft_lat = _dft_tables(s_lat)

    new_kv = []
    for l in range(depth):
        i = l // 2
        mod_l = mod[l]
        if l % 2 == 0:
            proj = _in_proj(rows, x, mod_l, norm_mix[l], w_in_even[i].astype(BF16))
            qc, kc, vc, kf, vf = _prep(proj, a_width, b_heads, b_kv, q_norm_b[i], k_norm_b[i],
                                       0, rows.tc, None, True, False, q_scale)
            q, k, v = _prep(proj, a_width, b_heads, b_kv, q_norm_b[i], k_norm_b[i],
                            rows.tc, rows.tl, rope, False, False, q_scale)
            sink = jnp.broadcast_to(sink_b[i][:, None, None], (b_heads, 1, HEAD_DIM))
            att = _ctx_attention(rows, qc, kc, vc, sink, b_heads * HEAD_DIM)
            att = _window_attention(rows, q, k, v, _cache_heads(cache_b_k[:, i]),
                                    _cache_heads(cache_b_v[:, i]), sink, att)
            uc, us = _chan_dft(proj, a_width, chan_cs)
            fm = _seq_dft(uc, us, dft_ctx[0], dft_ctx[1], 0, n_ctx_seq, None)
            fm = _seq_dft(uc, us, dft_lat[0], dft_lat[1], rows.tc, n_lat_seq, fm)
            x = _out_proj(rows, [fm, att], w_out_even[i].astype(BF16), x, mod_l)
            nkv = b_kv
        else:
            proj = _in_proj(rows, x, mod_l, norm_mix[l], w_in_odd[i].astype(BF16))
            qc, kc, vc, kf, vf = _prep(proj, 0, c_heads, c_kv, q_norm_c[i], k_norm_c[i],
                                       0, rows.tc, None, True, False, q_scale)
            q, k, vt = _prep(proj, 0, c_heads, c_kv, q_norm_c[i], k_norm_c[i],
                             rows.tc, rows.tl, rope, False, True, q_scale * LOG2E)
            att = _ctx_attention(rows, qc, kc, vc, None, c_heads * HEAD_DIM)
            att = _dense_attention(rows, q, k, vt, _cache_heads(cache_c_k[:, i]),
                                   _cache_heads(cache_c_v[:, i], True), att)
            x = _out_proj(rows, [att], w_out_odd[i].astype(BF16), x, mod_l)
            nkv = c_kv
        new_kv.append((kf.reshape(n_ctx_seq, s_ctx, nkv, HEAD_DIM),
                       vf.reshape(n_ctx_seq, s_ctx, nkv, HEAD_DIM)))
        x = _moe(rows, x, mod_l, norm_ffn[l], l, w_group[l], b_group[l], w_route[l], b_route[l],
                 w_gate, w_up, w_down)

    y_prompt = x[:rows.tc].reshape(n_ctx_seq, s_ctx, d)
    y_sample = x[rows.tc:].reshape(n_lat_seq, s_lat, d)
    new_b_k = jnp.stack([new_kv[l][0] for l in range(0, depth, 2)], axis=1)
    new_b_v = jnp.stack([new_kv[l][1] for l in range(0, depth, 2)], axis=1)
    new_c_k = jnp.stack([new_kv[l][0] for l in range(1, depth, 2)], axis=1)
    new_c_v = jnp.stack([new_kv[l][1] for l in range(1, depth, 2)], axis=1)
    return (y_prompt, y_sample, new_b_k, new_b_v, new_c_k, new_c_v)
```

```python
import functools
import math

import jax
import jax.numpy as jnp
from jax import lax
from jax.experimental import pallas as pl
from jax.experimental.pallas import tpu as pltpu

F32 = jnp.float32
BF16 = jnp.bfloat16

HEAD_DIM = 128
GRID_W = 64
ROPE_THETA = 10000.0
WINDOW = 128
RMS_EPS = 1e-6
NEG_INF = -1e30
A_GROUPS = 4
N_GROUPS = 8
TOP_K = 2
GQA_GROUP = 4
MOD_ROWS = 8
LOGIT_LANES = 128
ROW_TILE = 512
MOE_ROWS = 256
LOG2E = math.log2(math.e)
MIB = 1024 * 1024


def _cparams(sem, vmem_mib):
    return pltpu.CompilerParams(dimension_semantics=sem, vmem_limit_bytes=vmem_mib * MIB)


def _silu(x):
    return x / (1.0 + jnp.exp(-x))


def _dot(a, b):
    return jnp.dot(a, b, preferred_element_type=F32)


def _dot_t(a, b):
    return lax.dot_general(a, b, (((1,), (1,)), ((), ())), preferred_element_type=F32)


def _mod_kernel(c_ref, w_ref, b_ref, o_ref):
    s = _silu(c_ref[...]).astype(BF16)
    o_ref[0] = _dot(s, w_ref[0].astype(BF16)) + b_ref[0]


def _modulation(cvec, w_mod, b_mod):
    depth, d, n6 = w_mod.shape
    tn = 1024
    return pl.pallas_call(
        _mod_kernel,
        grid=(depth, n6 // tn),
        in_specs=[pl.BlockSpec((MOD_ROWS, d), lambda l, j: (0, 0)),
                  pl.BlockSpec((1, d, tn), lambda l, j: (l, 0, j)),
                  pl.BlockSpec((1, 1, tn), lambda l, j: (l, 0, j))],
        out_specs=pl.BlockSpec((1, MOD_ROWS, tn), lambda l, j: (l, 0, j)),
        out_shape=jax.ShapeDtypeStruct((depth, MOD_ROWS, n6), F32),
        compiler_params=_cparams(("arbitrary", "arbitrary"), 40),
        name="modulation",
    )(cvec, w_mod, b_mod.reshape(depth, 1, n6))


class _Rows:
    def __init__(self, n_ctx_seq, s_ctx, n_lat_seq, s_lat):
        self.n_ctx_seq, self.s_ctx, self.n_lat_seq, self.s_lat = n_ctx_seq, s_ctx, n_lat_seq, s_lat
        self.tc = n_ctx_seq * s_ctx
        self.tl = n_lat_seq * s_lat
        self.t = self.tc + self.tl

    def mod_row(self, i, tm):
        nct = self.tc // tm
        per = self.s_lat // tm
        return jnp.where(i < nct, 0, 1 + (i - nct) // per)


def _norm_mod(x, g, scale, shift):
    ms = jnp.mean(x * x, axis=-1, keepdims=True)
    y = x * lax.rsqrt(ms + RMS_EPS) * g
    return y * (1.0 + scale) + shift


def _in_proj_kernel(x_ref, mod_ref, g_ref, w_ref, o_ref, h_ref):
    i = pl.program_id(0)
    slot = i % 2

    @pl.when(i == 0)
    def _():
        h_ref[1] = jnp.zeros(h_ref.shape[1:], BF16)

    h = _norm_mod(x_ref[...], g_ref[...], mod_ref[0, 1:2, :], mod_ref[0, 0:1, :])
    o_ref[...] = _dot(h_ref[1 - slot], w_ref[...])
    h_ref[slot] = h.astype(BF16)


def _in_proj(rows, x, mod_l, g, w_bf16):
    t, d = x.shape
    n = w_bf16.shape[1]
    tm = ROW_TILE
    nt = t // tm

    def cur(i):
        return jnp.minimum(i, nt - 1)

    return pl.pallas_call(
        _in_proj_kernel,
        grid=(nt + 1,),
        in_specs=[pl.BlockSpec((tm, d), lambda i: (cur(i), 0)),
                  pl.BlockSpec((1, 6, d), lambda i: (rows.mod_row(cur(i), tm), 0, 0)),
                  pl.BlockSpec((1, d), lambda i: (0, 0)),
                  pl.BlockSpec((d, n), lambda i: (0, 0))],
        out_specs=pl.BlockSpec((tm, n), lambda i: (jnp.maximum(i - 1, 0), 0)),
        out_shape=jax.ShapeDtypeStruct((t, n), F32),
        scratch_shapes=[pltpu.VMEM((2, tm, d), BF16)],
        compiler_params=_cparams(("arbitrary",), 56),
        name="in_proj",
    )(x, mod_l, g.reshape(1, d), w_bf16)


def _swap32(y):
    lane = lax.broadcasted_iota(jnp.int32, y.shape, 1)
    return jnp.where((lane & 63) < 32, pltpu.roll(y, 96, 1), pltpu.roll(y, 32, 1))


def _prep_kernel(*refs, nq, nkv, rope, emit_f32, transposed, q_scale):
    q_ref, k_ref, v_ref, qn_ref, kn_ref = refs[:5]
    pos = 5
    if rope:
        cos_ref, sin_ref = refs[pos:pos + 2]
        pos += 2
    qo_ref, ko_ref, vo_ref = refs[pos:pos + 3]
    pos += 3
    if emit_f32:
        kf_ref, vf_ref = refs[pos:pos + 2]

    def head_norm(x, gain):
        ms = jnp.mean(x * x, axis=-1, keepdims=True)
        return x * lax.rsqrt(ms + RMS_EPS) * gain

    def rot(y):
        if not rope:
            return y
        return y * cos_ref[...] + _swap32(y) * sin_ref[...]

    for h in range(nq):
        sl = slice(h * HEAD_DIM, (h + 1) * HEAD_DIM)
        y = rot(head_norm(q_ref[:, sl], qn_ref[...])) * q_scale
        qo_ref[h] = y.astype(BF16)
    for h in range(nkv):
        sl = slice(h * HEAD_DIM, (h + 1) * HEAD_DIM)
        y = head_norm(k_ref[:, sl], kn_ref[...])
        v = v_ref[:, sl]
        if emit_f32:
            kf_ref[:, sl] = y
            vf_ref[:, sl] = v
        ko_ref[h] = rot(y).astype(BF16)
        vo_ref[h] = (v.T if transposed else v).astype(BF16)


def _prep(proj, col0, nq, nkv, qn, kn, row0, n_rows, rope_tabs, emit_f32, transposed, q_scale):
    tm = 256
    rope = rope_tabs is not None
    r0 = row0 // tm
    qw, kw = nq * HEAD_DIM, nkv * HEAD_DIM
    qb, kb, vb = col0 // qw, (col0 + qw) // kw, (col0 + qw + kw) // kw
    assert col0 % qw == 0 and (col0 + qw) % kw == 0 and row0 % tm == 0
    in_specs = [pl.BlockSpec((tm, qw), lambda i: (r0 + i, qb)),
                pl.BlockSpec((tm, kw), lambda i: (r0 + i, kb)),
                pl.BlockSpec((tm, kw), lambda i: (r0 + i, vb)),
                pl.BlockSpec((1, HEAD_DIM), lambda i: (0, 0)),
                pl.BlockSpec((1, HEAD_DIM), lambda i: (0, 0))]
    args = [proj, proj, proj, qn.reshape(1, HEAD_DIM), kn.reshape(1, HEAD_DIM)]
    if rope:
        per = rope_tabs[0].shape[0] // tm
        in_specs += [pl.BlockSpec((tm, HEAD_DIM), lambda i: (i % per, 0)),
                     pl.BlockSpec((tm, HEAD_DIM), lambda i: (i % per, 0))]
        args += list(rope_tabs)
    row_major = pl.BlockSpec((nq, tm, HEAD_DIM), lambda i: (0, i, 0))
    kv_row = pl.BlockSpec((nkv, tm, HEAD_DIM), lambda i: (0, i, 0))
    kv_col = pl.BlockSpec((nkv, HEAD_DIM, tm), lambda i: (0, 0, i))
    if transposed:
        out_specs = [row_major, kv_row, kv_col]
        out_shape = [jax.ShapeDtypeStruct((nq, n_rows, HEAD_DIM), BF16),
                     jax.ShapeDtypeStruct((nkv, n_rows, HEAD_DIM), BF16),
                     jax.ShapeDtypeStruct((nkv, HEAD_DIM, n_rows), BF16)]
    else:
        out_specs = [row_major, kv_row, kv_row]
        out_shape = [jax.ShapeDtypeStruct((nq, n_rows, HEAD_DIM), BF16),
                     jax.ShapeDtypeStruct((nkv, n_rows, HEAD_DIM), BF16),
                     jax.ShapeDtypeStruct((nkv, n_rows, HEAD_DIM), BF16)]
    if emit_f32:
        out_specs += [pl.BlockSpec((tm, kw), lambda i: (i, 0)),
                      pl.BlockSpec((tm, kw), lambda i: (i, 0))]
        out_shape += [jax.ShapeDtypeStruct((n_rows, kw), F32),
                      jax.ShapeDtypeStruct((n_rows, kw), F32)]
    return pl.pallas_call(
        functools.partial(_prep_kernel, nq=nq, nkv=nkv, rope=rope, emit_f32=emit_f32,
                          transposed=transposed, q_scale=q_scale),
        grid=(n_rows // tm,),
        in_specs=in_specs,
        out_specs=out_specs,
        out_shape=out_shape,
        compiler_params=_cparams(("arbitrary",), 40),
        name="qkv_prep_rope" if rope else "qkv_prep",
    )(*args)


def _softmax_pv(s, v_list, widths, sink):
    m = jnp.max(s, axis=-1, keepdims=True)
    if sink is not None:
        m = jnp.maximum(m, sink)
    p = jnp.exp(s - m)
    l = jnp.sum(p, axis=-1, keepdims=True)
    if sink is not None:
        l = l + jnp.exp(sink - m)
    pb = p.astype(BF16)
    o = None
    c0 = 0
    for v, w in zip(v_list, widths):
        part = _dot(pb[:, c0:c0 + w], v)
        o = part if o is None else o + part
        c0 += w
    return o / l


def _ctx_attn_kernel(*refs, has_sink):
    if has_sink:
        q_ref, k_ref, v_ref, sink_ref, o_ref = refs
    else:
        q_ref, k_ref, v_ref, o_ref = refs
    k = k_ref[0]
    v = v_ref[0]
    for g in range(GQA_GROUP):
        s = _dot_t(q_ref[g], k)
        sink = sink_ref[g][:, 0:1] if has_sink else None
        o = _softmax_pv(s, [v], [k.shape[0]], sink)
        o_ref[:, g * HEAD_DIM:(g + 1) * HEAD_DIM] = o.astype(BF16)


def _ctx_attention(rows, q, k, v, sink, att_cols):
    nkv = k.shape[0]
    s = rows.s_ctx
    gw = GQA_GROUP * HEAD_DIM
    has_sink = sink is not None
    in_specs = [pl.BlockSpec((GQA_GROUP, s, HEAD_DIM), lambda b, h: (h, b, 0)),
                pl.BlockSpec((1, s, HEAD_DIM), lambda b, h: (h, b, 0)),
                pl.BlockSpec((1, s, HEAD_DIM), lambda b, h: (h, b, 0))]
    args = [q, k, v]
    if has_sink:
        in_specs.append(pl.BlockSpec((GQA_GROUP, 1, HEAD_DIM), lambda b, h: (h, 0, 0)))
        args.append(sink)
    return pl.pallas_call(
        functools.partial(_ctx_attn_kernel, has_sink=has_sink),
        grid=(rows.n_ctx_seq, nkv),
        in_specs=in_specs,
        out_specs=pl.BlockSpec((s, gw), lambda b, h: (b, h)),
        out_shape=jax.ShapeDtypeStruct((rows.t, att_cols), BF16),
        compiler_params=_cparams(("arbitrary", "arbitrary"), 40),
        name="ctx_attention",
    )(*args)


def _window_attn_kernel(q_ref, kp_ref, kc_ref, kn_ref, vp_ref, vc_ref, vn_ref, kx_ref, vx_ref,
                        sink_ref, prev_ref, o_ref, *, nkv, n_blocks):
    del prev_ref
    n = pl.program_id(1)
    qb = WINDOW
    m_rows = GQA_GROUP * qb
    r = lax.broadcasted_iota(jnp.int32, (m_rows, qb), 0) & (qb - 1)
    c = lax.broadcasted_iota(jnp.int32, (m_rows, qb), 1)
    ok_prev = (c >= r) & (n > 0)
    ok_next = (c <= r) & (n < n_blocks - 1)
    n_ctx = kx_ref.shape[2]
    for h in range(nkv):
        q = q_ref[h * GQA_GROUP:(h + 1) * GQA_GROUP].reshape(m_rows, HEAD_DIM)
        s = jnp.concatenate(
            [jnp.where(ok_prev, _dot_t(q, kp_ref[h]), NEG_INF),
             _dot_t(q, kc_ref[h]),
             jnp.where(ok_next, _dot_t(q, kn_ref[h]), NEG_INF),
             _dot_t(q, kx_ref[0, h])], axis=-1)
        sink = jnp.concatenate(
            [jnp.broadcast_to(sink_ref[h * GQA_GROUP + g][:, 0:1], (qb, 1)) for g in range(GQA_GROUP)],
            axis=0)
        o = _softmax_pv(s, [vp_ref[h], vc_ref[h], vn_ref[h], vx_ref[0, h]], [qb, qb, qb, n_ctx], sink)
        for g in range(GQA_GROUP):
            col = (h * GQA_GROUP + g) * HEAD_DIM
            o_ref[:, col:col + HEAD_DIM] = o[g * qb:(g + 1) * qb].astype(BF16)


def _window_attention(rows, q, k, v, kx, vx, sink, prev):
    nq, nkv = q.shape[0], k.shape[0]
    qb = WINDOW
    nb = rows.s_lat // qb
    out0 = rows.tc // qb
    n_ctx = kx.shape[2]

    def kv_spec(shift):
        def imap(b, n):
            return (0, b * nb + jnp.clip(n + shift, 0, nb - 1), 0)
        return pl.BlockSpec((nkv, qb, HEAD_DIM), imap)

    cache_spec = pl.BlockSpec((1, nkv, n_ctx, HEAD_DIM), lambda b, n: (b, 0, 0, 0))
    return pl.pallas_call(
        functools.partial(_window_attn_kernel, nkv=nkv, n_blocks=nb),
        grid=(rows.n_lat_seq, nb),
        in_specs=[pl.BlockSpec((nq, qb, HEAD_DIM), lambda b, n: (0, b * nb + n, 0)),
                  kv_spec(-1), kv_spec(0), kv_spec(1),
                  kv_spec(-1), kv_spec(0), kv_spec(1),
                  cache_spec, cache_spec,
                  pl.BlockSpec((nq, 1, HEAD_DIM), lambda b, n: (0, 0, 0)),
                  pl.BlockSpec(memory_space=pl.ANY)],
        out_specs=pl.BlockSpec((qb, nq * HEAD_DIM), lambda b, n: (out0 + b * nb + n, 0)),
        out_shape=jax.ShapeDtypeStruct(prev.shape, BF16),
        input_output_aliases={10: 0},
        compiler_params=_cparams(("arbitrary", "arbitrary"), 40),
        name="window_attention",
    )(q, k, k, k, v, v, v, kx, vx, sink, prev)


def _dense_attn_kernel(q_ref, k_ref, vt_ref, kx_ref, vxt_ref, prev_ref, o_ref, *, bq, bk):
    del prev_ref
    m_cols = GQA_GROUP * bq
    q = q_ref[...].reshape(m_cols, HEAD_DIM)

    def update(k, vt, carry):
        m_old, l_old, acc = carry
        s = _dot_t(k, q)
        m_new = jnp.maximum(m_old, jnp.max(s, axis=0, keepdims=True))
        alpha = jnp.exp2(m_old - m_new)
        p = jnp.exp2(s - m_new)
        l_new = alpha * l_old + jnp.sum(p, axis=0, keepdims=True)
        acc = alpha * acc + _dot(vt, p.astype(BF16))
        return m_new, l_new, acc

    carry = (jnp.full((1, m_cols), NEG_INF, F32), jnp.zeros((1, m_cols), F32),
             jnp.zeros((HEAD_DIM, m_cols), F32))
    for c in range(k_ref.shape[1] // bk):
        carry = update(k_ref[0, c * bk:(c + 1) * bk, :], vt_ref[0, :, c * bk:(c + 1) * bk], carry)
    _, l, acc = update(kx_ref[0, 0], vxt_ref[0, 0], carry)
    o = acc / l
    for g in range(GQA_GROUP):
        o_ref[:, g * HEAD_DIM:(g + 1) * HEAD_DIM] = o[:, g * bq:(g + 1) * bq].T.astype(BF16)


def _dense_attention(rows, q, k, vt, kx, vxt, prev):
    nkv = k.shape[0]
    bq, bk = 256, 512
    s = rows.s_lat
    assert s % bk == 0 and s % bq == 0
    nqb = s // bq
    out0 = rows.tc // bq
    n_ctx = kx.shape[2]
    return pl.pallas_call(
        functools.partial(_dense_attn_kernel, bq=bq, bk=bk),
        grid=(rows.n_lat_seq, nkv, nqb),
        in_specs=[pl.BlockSpec((GQA_GROUP, bq, HEAD_DIM), lambda b, h, i: (h, b * nqb + i, 0)),
                  pl.BlockSpec((1, s, HEAD_DIM), lambda b, h, i: (h, b, 0)),
                  pl.BlockSpec((1, HEAD_DIM, s), lambda b, h, i: (h, 0, b)),
                  pl.BlockSpec((1, 1, n_ctx, HEAD_DIM), lambda b, h, i: (b, h, 0, 0)),
                  pl.BlockSpec((1, 1, HEAD_DIM, n_ctx), lambda b, h, i: (b, h, 0, 0)),
                  pl.BlockSpec(memory_space=pl.ANY)],
        out_specs=pl.BlockSpec((bq, GQA_GROUP * HEAD_DIM), lambda b, h, i: (out0 + b * nqb + i, h)),
        out_shape=jax.ShapeDtypeStruct(prev.shape, BF16),
        input_output_aliases={5: 0},
        compiler_params=_cparams(("arbitrary", "arbitrary", "arbitrary"), 48),
        name="dense_attention",
    )(q, k, vt, kx, vxt, prev)


def _dft_tables(n):
    if n % 64 != 0 or n < 128:
        j = jnp.arange(n, dtype=jnp.int32)
        ang = ((j[:, None] * j[None, :]) % n).astype(F32) * (2.0 * math.pi / n)
        scale = n ** -0.5
        return (jnp.cos(ang) * scale).astype(BF16), (jnp.sin(ang) * scale).astype(BF16)
    rows = n // 64
    k = jnp.arange(n, dtype=jnp.int32)[None, :]
    j1 = jnp.arange(rows, dtype=jnp.int32)[:, None]
    j2 = jnp.arange(64, dtype=jnp.int32)[:, None]
    a1 = ((j1 * k) % rows).astype(F32) * (2.0 * math.pi / rows)
    a2 = ((j2 * k) % n).astype(F32) * (2.0 * math.pi / n)
    scale = n ** -0.5
    c1, s1 = jnp.cos(a1)[:, None, :], jnp.sin(a1)[:, None, :]
    c2, s2 = (jnp.cos(a2) * scale)[None], (jnp.sin(a2) * scale)[None]
    cos = (c1 * c2 - s1 * s2).reshape(n, n)
    sin = (s1 * c2 + c1 * s2).reshape(n, n)
    return cos.astype(BF16), sin.astype(BF16)


def _chan_dft_kernel(a_ref, cs_ref, uc_ref, us_ref, *, gd):
    for g in range(A_GROUPS):
        sl = slice(g * gd, (g + 1) * gd)
        y = _dot(a_ref[:, sl].astype(BF16), cs_ref[...])
        uc_ref[:, sl] = y[:, :gd].astype(BF16)
        us_ref[:, sl] = y[:, gd:].astype(BF16)


def _chan_dft(proj, a_width, cs):
    t = proj.shape[0]
    gd = a_width // A_GROUPS
    tm = ROW_TILE
    return pl.pallas_call(
        functools.partial(_chan_dft_kernel, gd=gd),
        grid=(t // tm,),
        in_specs=[pl.BlockSpec((tm, a_width), lambda i: (i, 0)),
                  pl.BlockSpec((gd, 2 * gd), lambda i: (0, 0))],
        out_specs=[pl.BlockSpec((tm, a_width), lambda i: (i, 0)),
                   pl.BlockSpec((tm, a_width), lambda i: (i, 0))],
        out_shape=[jax.ShapeDtypeStruct((t, a_width), BF16)] * 2,
        compiler_params=_cparams(("arbitrary",), 40),
        name="channel_dft",
    )(proj, cs)


def _seq_dft_kernel(c_ref, s_ref, uc_ref, us_ref, *rest):
    o_ref = rest[-1]
    o_ref[...] = (_dot(c_ref[...], uc_ref[...]) - _dot(s_ref[...], us_ref[...])).astype(BF16)


def _seq_dft(uc, us, cmat, smat, row0, n_seq, prev):
    t, width = uc.shape
    s = cmat.shape[0]
    tm = min(s, ROW_TILE)
    tn = 512
    ni = s // tm
    seq0 = row0 // s
    out0 = row0 // tm
    in_specs = [pl.BlockSpec((tm, s), lambda b, j, i: (i, 0)),
                pl.BlockSpec((tm, s), lambda b, j, i: (i, 0)),
                pl.BlockSpec((s, tn), lambda b, j, i: (seq0 + b, j)),
                pl.BlockSpec((s, tn), lambda b, j, i: (seq0 + b, j))]
    args = [cmat, smat, uc, us]
    aliases = {}
    if prev is not None:
        in_specs.append(pl.BlockSpec(memory_space=pl.ANY))
        args.append(prev)
        aliases = {4: 0}
    return pl.pallas_call(
        _seq_dft_kernel,
        grid=(n_seq, width // tn, ni),
        in_specs=in_specs,
        out_specs=pl.BlockSpec((tm, tn), lambda b, j, i: (out0 + b * ni + i, j)),
        out_shape=jax.ShapeDtypeStruct((t, width), BF16),
        input_output_aliases=aliases,
        compiler_params=_cparams(("arbitrary", "arbitrary", "arbitrary"), 48),
        name="sequence_dft",
    )(*args)


def _out_proj_kernel(*refs, n_lhs):
    lhs = refs[:n_lhs]
    ws = refs[n_lhs:2 * n_lhs]
    x_ref, mod_ref, o_ref = refs[2 * n_lhs:]
    acc = None
    for a, w in zip(lhs, ws):
        part = _dot(a[...], w[...])
        acc = part if acc is None else acc + part
    o_ref[...] = x_ref[...] + mod_ref[0, 2:3, :] * acc


def _out_proj(rows, lhs_list, w_bf16, x, mod_l):
    t, d = x.shape
    tm = ROW_TILE
    in_specs, args, w_args, w_specs = [], [], [], []
    k0 = 0
    for a in lhs_list:
        kw = a.shape[1]
        in_specs.append(pl.BlockSpec((tm, kw), lambda i: (i, 0)))
        args.append(a)
        kb = k0 // kw
        assert k0 % kw == 0
        w_specs.append(pl.BlockSpec((kw, d), lambda i, kb=kb: (kb, 0)))
        w_args.append(w_bf16)
        k0 += kw
    in_specs += w_specs + [pl.BlockSpec((tm, d), lambda i: (i, 0)),
                           pl.BlockSpec((1, 6, d), lambda i: (rows.mod_row(i, tm), 0, 0))]
    args += w_args + [x, mod_l]
    return pl.pallas_call(
        functools.partial(_out_proj_kernel, n_lhs=len(lhs_list)),
        grid=(t // tm,),
        in_specs=in_specs,
        out_specs=pl.BlockSpec((tm, d), lambda i: (i, 0)),
        out_shape=jax.ShapeDtypeStruct((t, d), F32),
        compiler_params=_cparams(("arbitrary",), 56),
        name="out_proj",
    )(*args)


def _ffn_norm_kernel(x_ref, mod_ref, g_ref, whi_ref, wlo_ref, b_ref, h_ref, lg_ref):
    h = _norm_mod(x_ref[...], g_ref[...], mod_ref[0, 4:5, :], mod_ref[0, 3:4, :])
    hi = h.astype(BF16)
    lo = (h - hi.astype(F32)).astype(BF16)
    h_ref[...] = h
    lg_ref[...] = (_dot(hi, whi_ref[...]) + _dot(hi, wlo_ref[...]) + _dot(lo, whi_ref[...])
                   + b_ref[...])


def _ffn_norm_router(rows, x, mod_l, g, w_hi, w_lo, b_r):
    t, d = x.shape
    tm = ROW_TILE
    return pl.pallas_call(
        _ffn_norm_kernel,
        grid=(t // tm,),
        in_specs=[pl.BlockSpec((tm, d), lambda i: (i, 0)),
                  pl.BlockSpec((1, 6, d), lambda i: (rows.mod_row(i, tm), 0, 0)),
                  pl.BlockSpec((1, d), lambda i: (0, 0)),
                  pl.BlockSpec((d, LOGIT_LANES), lambda i: (0, 0)),
                  pl.BlockSpec((d, LOGIT_LANES), lambda i: (0, 0)),
                  pl.BlockSpec((1, LOGIT_LANES), lambda i: (0, 0))],
        out_specs=[pl.BlockSpec((tm, d), lambda i: (i, 0)),
                   pl.BlockSpec((tm, LOGIT_LANES), lambda i: (i, 0))],
        out_shape=[jax.ShapeDtypeStruct((t, d), F32),
                   jax.ShapeDtypeStruct((t, LOGIT_LANES), F32)],
        compiler_params=_cparams(("arbitrary",), 40),
        name="ffn_norm_router",
    )(x, mod_l, g.reshape(1, d), w_hi, w_lo, b_r)


def _expert_kernel(be_ref, nb_ref, src_ref, nsrc_ref, dk_ref, dt_ref, h_hbm, wg_ref, wu_ref, wd_ref,
                   y_hbm, xbuf, ybuf, wgb, wub, wdb, gsem, ssem):
    b = pl.program_id(0)
    n_used = nb_ref[0]
    bm = xbuf.shape[1]
    slot = b % 2

    def gather_copy(ref, r, dst_slot):
        return pltpu.make_async_copy(h_hbm.at[pl.ds(ref[0, 0, r], 1)], xbuf.at[dst_slot, pl.ds(r, 1)],
                                     gsem.at[dst_slot])

    def scatter_copy(r):
        return pltpu.make_async_copy(ybuf.at[pl.ds(r, 1)],
                                     y_hbm.at[dk_ref[0, 0, r], pl.ds(dt_ref[0, 0, r], 1)], ssem.at[0])

    @pl.when(b == 0)
    def _():
        for r in range(bm):
            gather_copy(src_ref, r, 0).start()

    @pl.when(b < n_used)
    def _():
        for r in range(bm):
            gather_copy(src_ref, r, slot).wait()

        @pl.when((b == 0) | (be_ref[b] != be_ref[jnp.maximum(b - 1, 0)]))
        def _():
            wgb[...] = wg_ref[0, 0].astype(BF16)
            wub[...] = wu_ref[0, 0].astype(BF16)
            wdb[...] = wd_ref[0, 0].astype(BF16)

        for r in range(bm):
            gather_copy(nsrc_ref, r, 1 - slot).start()
        x = xbuf[slot].astype(BF16)
        hid = (_silu(_dot(x, wgb[...])) * _dot(x, wub[...])).astype(BF16)

        @pl.when(b > 0)
        def _():
            for r in range(bm):
                scatter_copy(r).wait()

        ybuf[...] = _dot(hid, wdb[...])
        for r in range(bm):
            scatter_copy(r).start()

        @pl.when(b == n_used - 1)
        def _():
            for r in range(bm):
                scatter_copy(r).wait()
            for r in range(bm):
                gather_copy(nsrc_ref, r, 1 - slot).wait()


def _experts(h, src, dst_k, dst_t, blk_expert, n_used, layer, w_gate, w_up, w_down):
    t, d = h.shape
    de = w_gate.shape[3]
    bm = MOE_ROWS
    nblk = src.shape[0]

    def w_map(b, be, nb):
        return (layer, be[b], 0, 0)

    def cur(b, be, nb):
        return (b, 0, 0)

    def nxt(b, be, nb):
        return (jnp.minimum(b + 1, jnp.maximum(nb[0] - 1, 0)), 0, 0)

    def smem(imap):
        return pl.BlockSpec((1, 1, bm), imap, memory_space=pltpu.SMEM)

    grid_spec = pltpu.PrefetchScalarGridSpec(
        num_scalar_prefetch=2,
        grid=(nblk,),
        in_specs=[smem(cur), smem(nxt), smem(cur), smem(cur),
                  pl.BlockSpec(memory_space=pl.ANY),
                  pl.BlockSpec((1, 1, d, de), w_map),
                  pl.BlockSpec((1, 1, d, de), w_map),
                  pl.BlockSpec((1, 1, de, d), w_map)],
        out_specs=pl.BlockSpec(memory_space=pl.ANY),
        scratch_shapes=[pltpu.VMEM((2, bm, d), F32), pltpu.VMEM((bm, d), F32),
                        pltpu.VMEM((d, de), BF16), pltpu.VMEM((d, de), BF16),
                        pltpu.VMEM((de, d), BF16),
                        pltpu.SemaphoreType.DMA((2,)), pltpu.SemaphoreType.DMA((1,))],
    )
    return pl.pallas_call(
        _expert_kernel,
        grid_spec=grid_spec,
        out_shape=jax.ShapeDtypeStruct((TOP_K, t + bm, d), F32),
        compiler_params=_cparams(("arbitrary",), 56),
        name="moe_experts",
    )(blk_expert, n_used, src, src, dst_k, dst_t, h, w_gate, w_up, w_down)


def _combine_kernel(x_ref, y0_ref, y1_ref, w_ref, mod_ref, o_ref):
    ff = w_ref[:, 0:1] * y0_ref[0] + w_ref[:, 1:2] * y1_ref[0]
    o_ref[...] = x_ref[...] + mod_ref[0, 5:6, :] * ff


def _combine(rows, x, y, wts, mod_l):
    t, d = x.shape
    tm = 256
    return pl.pallas_call(
        _combine_kernel,
        grid=(t // tm,),
        in_specs=[pl.BlockSpec((tm, d), lambda i: (i, 0)),
                  pl.BlockSpec((1, tm, d), lambda i: (0, i, 0)),
                  pl.BlockSpec((1, tm, d), lambda i: (1, i, 0)),
                  pl.BlockSpec((tm, TOP_K), lambda i: (i, 0)),
                  pl.BlockSpec((1, 6, d), lambda i: (rows.mod_row(i, tm), 0, 0))],
        out_specs=pl.BlockSpec((tm, d), lambda i: (i, 0)),
        out_shape=jax.ShapeDtypeStruct((t, d), F32),
        compiler_params=_cparams(("arbitrary",), 40),
        name="moe_combine",
    )(x, y, y, wts, mod_l)


def _moe(rows, x, mod_l, g, layer, w_group, b_group, w_route, b_route, w_gate, w_up, w_down):
    t, d = x.shape
    n_exp = w_route.shape[1]
    per_group = n_exp // N_GROUPS
    pad = LOGIT_LANES - N_GROUPS - n_exp
    w_r = jnp.concatenate([w_group, w_route, jnp.zeros((d, pad), F32)], axis=1)
    b_r = jnp.concatenate([b_group, b_route, jnp.zeros((pad,), F32)]).reshape(1, LOGIT_LANES)
    w_hi = w_r.astype(BF16)
    w_lo = (w_r - w_hi.astype(F32)).astype(BF16)
    h, logits = _ffn_norm_router(rows, x, mod_l, g, w_hi, w_lo, b_r)

    g_logit = logits[:, :N_GROUPS]
    g_idx = jnp.argmax(g_logit, axis=-1).astype(jnp.int32)
    g_w = 1.0 / jnp.sum(jnp.exp(g_logit - jnp.max(g_logit, axis=-1, keepdims=True)), axis=-1, keepdims=True)
    e_all = logits[:, N_GROUPS:N_GROUPS + n_exp].reshape(t, N_GROUPS, per_group)
    e_logit = jnp.take_along_axis(e_all, g_idx[:, None, None], axis=1)[:, 0]
    prob = jax.nn.softmax(e_logit, axis=-1)
    lane = lax.broadcasted_iota(jnp.int32, prob.shape, 1)
    i1 = jnp.argmax(prob, axis=-1).astype(jnp.int32)[:, None]
    p1 = jnp.max(prob, axis=-1, keepdims=True)
    rest = jnp.where(lane == i1, -1.0, prob)
    i2 = jnp.argmax(rest, axis=-1).astype(jnp.int32)[:, None]
    p2 = jnp.max(rest, axis=-1, keepdims=True)
    top_p = jnp.concatenate([p1, p2], axis=1)
    top_i = jnp.concatenate([i1, i2], axis=1)
    wts = g_w * top_p / jnp.sum(top_p, axis=-1, keepdims=True)
    eid = (g_idx[:, None] * per_group + top_i).reshape(-1).astype(jnp.int32)

    bm = MOE_ROWS
    n_assign = t * TOP_K
    order = jnp.argsort(eid).astype(jnp.int32)
    e_sorted = eid[order]
    bounds = jnp.searchsorted(e_sorted, jnp.arange(n_exp + 1, dtype=jnp.int32), side='left').astype(jnp.int32)
    start, sizes = bounds[:-1], bounds[1:] - bounds[:-1]
    padded = ((sizes + bm - 1) // bm) * bm
    pad_end = jnp.cumsum(padded)
    pad_start = pad_end - padded
    nblk = -(-(n_assign + n_exp * (bm - 1)) // bm)
    blk_start = jnp.arange(nblk, dtype=jnp.int32) * bm
    blk_expert = jnp.minimum(jnp.searchsorted(pad_end, blk_start, side='right'), n_exp - 1).astype(jnp.int32)
    n_used = (pad_end[-1:] // bm).astype(jnp.int32)
    blk_off = blk_start - pad_start[blk_expert]
    n_valid = jnp.clip(sizes[blk_expert] - blk_off, 0, bm)
    r = jnp.arange(bm, dtype=jnp.int32)[None, :]
    pos = jnp.clip(start[blk_expert][:, None] + blk_off[:, None] + r, 0, n_assign - 1)
    valid = r < n_valid[:, None]
    flat = order[pos]
    shape3 = (nblk, 1, bm)
    spread = (blk_start[:, None] + r) % t
    src = jnp.where(valid, flat // TOP_K, spread).astype(jnp.int32).reshape(shape3)
    dst_k = jnp.where(valid, flat % TOP_K, 0).astype(jnp.int32).reshape(shape3)
    dst_t = jnp.where(valid, flat // TOP_K, t + r).astype(jnp.int32).reshape(shape3)
    y = _experts(h, src, dst_k, dst_t, blk_expert, n_used, layer, w_gate, w_up, w_down)
    return _combine(rows, x, y, wts, mod_l)


def _rope_tables(s_lat):
    n_rows = s_lat // GRID_W
    row = jnp.repeat(jnp.arange(n_rows, dtype=F32), GRID_W)
    col = jnp.tile(jnp.arange(GRID_W, dtype=F32), n_rows)
    axis_dim = HEAD_DIM // 2
    inv_freq = ROPE_THETA ** (-jnp.arange(0, axis_dim, 2, dtype=F32) / axis_dim)
    ar, ac = row[:, None] * inv_freq, col[:, None] * inv_freq
    cos = jnp.concatenate([jnp.cos(ar), jnp.cos(ar), jnp.cos(ac), jnp.cos(ac)], axis=1)
    sin = jnp.concatenate([-jnp.sin(ar), jnp.sin(ar), -jnp.sin(ac), jnp.sin(ac)], axis=1)
    return cos, sin


def _cache_heads(c, transposed=False):
    perm = (0, 2, 3, 1) if transposed else (0, 2, 1, 3)
    return jnp.transpose(c, perm).astype(BF16)


def kernel(x_prompt, x_sample, cache_b_k, cache_b_v, cache_c_k, cache_c_v, c, c_ctx,
           w_mod, b_mod, norm_mix, norm_ffn, w_in_even, w_out_even, q_norm_b, k_norm_b, sink_b,
           w_in_odd, w_out_odd, q_norm_c, k_norm_c, w_group, b_group, w_route, b_route,
           w_gate, w_up, w_down):
    n_ctx_seq, s_ctx, d = x_prompt.shape
    n_lat_seq, s_lat, _ = x_sample.shape
    depth = w_mod.shape[0]
    rows = _Rows(n_ctx_seq, s_ctx, n_lat_seq, s_lat)
    assert 1 + n_lat_seq <= MOD_ROWS and rows.tc % ROW_TILE == 0 and s_lat % ROW_TILE == 0
    a_width = A_GROUPS * (d // 8)
    b_heads = (d - a_width) // HEAD_DIM
    b_kv = b_heads // GQA_GROUP
    c_heads = d // HEAD_DIM
    c_kv = c_heads // GQA_GROUP
    q_scale = HEAD_DIM ** -0.5

    cvec = jnp.concatenate([c_ctx[None, :], c, jnp.zeros((MOD_ROWS - 1 - n_lat_seq, d), F32)], axis=0)
    mod = _modulation(cvec, w_mod, b_mod).reshape(depth, MOD_ROWS, 6, d)

    x = jnp.concatenate([x_prompt.reshape(rows.tc, d), x_sample.reshape(rows.tl, d)], axis=0)
    rope = _rope_tables(s_lat)
    gd = a_width // A_GROUPS
    cc, sc = _dft_tables(gd)
    chan_cs = jnp.concatenate([cc, sc], axis=1)
    dft_ctx = _dft_tables(s_ctx)
    dft_lat = _dft_tables(s_lat)

    new_kv = []
    for l in range(depth):
        i = l // 2
        mod_l = mod[l]
        if l % 2 == 0:
            proj = _in_proj(rows, x, mod_l, norm_mix[l], w_in_even[i].astype(BF16))
            qc, kc, vc, kf, vf = _prep(proj, a_width, b_heads, b_kv, q_norm_b[i], k_norm_b[i],
                                       0, rows.tc, None, True, False, q_scale)
            q, k, v = _prep(proj, a_width, b_heads, b_kv, q_norm_b[i], k_norm_b[i],
                            rows.tc, rows.tl, rope, False, False, q_scale)
            sink = jnp.broadcast_to(sink_b[i][:, None, None], (b_heads, 1, HEAD_DIM))
            att = _ctx_attention(rows, qc, kc, vc, sink, b_heads * HEAD_DIM)
            att = _window_attention(rows, q, k, v, _cache_heads(cache_b_k[:, i]),
                                    _cache_heads(cache_b_v[:, i]), sink, att)
            uc, us = _chan_dft(proj, a_width, chan_cs)
            fm = _seq_dft(uc, us, dft_ctx[0], dft_ctx[1], 0, n_ctx_seq, None)
            fm = _seq_dft(uc, us, dft_lat[0], dft_lat[1], rows.tc, n_lat_seq, fm)
            x = _out_proj(rows, [fm, att], w_out_even[i].astype(BF16), x, mod_l)
            nkv = b_kv
        else:
            proj = _in_proj(rows, x, mod_l, norm_mix[l], w_in_odd[i].astype(BF16))
            qc, kc, vc, kf, vf = _prep(proj, 0, c_heads, c_kv, q_norm_c[i], k_norm_c[i],
                                       0, rows.tc, None, True, False, q_scale)
            q, k, vt = _prep(proj, 0, c_heads, c_kv, q_norm_c[i], k_norm_c[i],
                             rows.tc, rows.tl, rope, False, True, q_scale * LOG2E)
            att = _ctx_attention(rows, qc, kc, vc, None, c_heads * HEAD_DIM)
            att = _dense_attention(rows, q, k, vt, _cache_heads(cache_c_k[:, i]),
                                   _cache_heads(cache_c_v[:, i], True), att)
            x = _out_proj(rows, [att], w_out_odd[i].astype(BF16), x, mod_l)
            nkv = c_kv
        new_kv.append((kf.reshape(n_ctx_seq, s_ctx, nkv, HEAD_DIM),
                       vf.reshape(n_ctx_seq, s_ctx, nkv, HEAD_DIM)))
        x = _moe(rows, x, mod_l, norm_ffn[l], l, w_group[l], b_group[l], w_route[l], b_route[l],
                 w_gate, w_up, w_down)

    y_prompt = x[:rows.tc].reshape(n_ctx_seq, s_ctx, d)
    y_sample = x[rows.tc:].reshape(n_lat_seq, s_lat, d)
    new_b_k = jnp.stack([new_kv[l][0] for l in range(0, depth, 2)], axis=1)
    new_b_v = jnp.stack([new_kv[l][1] for l in range(0, depth, 2)], axis=1)
    new_c_k = jnp.stack([new_kv[l][0] for l in range(1, depth, 2)], axis=1)
    new_c_v = jnp.stack([new_kv[l][1] for l in range(1, depth, 2)], axis=1)
    return (y_prompt, y_sample, new_b_k, new_b_v, new_c_k, new_c_v)
```

```python
import functools
import math

import jax
import jax.numpy as jnp
from jax import lax
from jax.experimental import pallas as pl
from jax.experimental.pallas import tpu as pltpu

F32 = jnp.float32
BF16 = jnp.bfloat16

HEAD_DIM = 128
GRID_W = 64
ROPE_THETA = 10000.0
WINDOW = 128
RMS_EPS = 1e-6
NEG_INF = -1e30
A_GROUPS = 4
N_GROUPS = 8
TOP_K = 2
GQA_GROUP = 4
MOD_ROWS = 8
LOGIT_LANES = 128
ROW_TILE = 512
MOE_ROWS = 256
LOG2E = math.log2(math.e)
MIB = 1024 * 1024


def _cparams(sem, vmem_mib):
    return pltpu.CompilerParams(dimension_semantics=sem, vmem_limit_bytes=vmem_mib * MIB)


def _silu(x):
    return x / (1.0 + jnp.exp(-x))


def _dot(a, b):
    return jnp.dot(a, b, preferred_element_type=F32)


def _dot_t(a, b):
    return lax.dot_general(a, b, (((1,), (1,)), ((), ())), preferred_element_type=F32)


def _mod_kernel(c_ref, w_ref, b_ref, o_ref):
    s = _silu(c_ref[...]).astype(BF16)
    o_ref[0] = _dot(s, w_ref[0].astype(BF16)) + b_ref[0]


def _modulation(cvec, w_mod, b_mod):
    depth, d, n6 = w_mod.shape
    tn = 1024
    return pl.pallas_call(
        _mod_kernel,
        grid=(depth, n6 // tn),
        in_specs=[pl.BlockSpec((MOD_ROWS, d), lambda l, j: (0, 0)),
                  pl.BlockSpec((1, d, tn), lambda l, j: (l, 0, j)),
                  pl.BlockSpec((1, 1, tn), lambda l, j: (l, 0, j))],
        out_specs=pl.BlockSpec((1, MOD_ROWS, tn), lambda l, j: (l, 0, j)),
        out_shape=jax.ShapeDtypeStruct((depth, MOD_ROWS, n6), F32),
        compiler_params=_cparams(("arbitrary", "arbitrary"), 40),
        name="modulation",
    )(cvec, w_mod, b_mod.reshape(depth, 1, n6))


class _Rows:
    def __init__(self, n_ctx_seq, s_ctx, n_lat_seq, s_lat):
        self.n_ctx_seq, self.s_ctx, self.n_lat_seq, self.s_lat = n_ctx_seq, s_ctx, n_lat_seq, s_lat
        self.tc = n_ctx_seq * s_ctx
        self.tl = n_lat_seq * s_lat
        self.t = self.tc + self.tl

    def mod_row(self, i, tm):
        nct = self.tc // tm
        per = self.s_lat // tm
        return jnp.where(i < nct, 0, 1 + (i - nct) // per)


def _norm_mod(x, g, scale, shift):
    ms = jnp.mean(x * x, axis=-1, keepdims=True)
    y = x * lax.rsqrt(ms + RMS_EPS) * g
    return y * (1.0 + scale) + shift


def _in_proj_kernel(x_ref, mod_ref, g_ref, w_ref, o_ref, h_ref):
    i = pl.program_id(0)
    slot = i % 2

    @pl.when(i == 0)
    def _():
        h_ref[1] = jnp.zeros(h_ref.shape[1:], BF16)

    h = _norm_mod(x_ref[...], g_ref[...], mod_ref[0, 1:2, :], mod_ref[0, 0:1, :])
    o_ref[...] = _dot(h_ref[1 - slot], w_ref[...])
    h_ref[slot] = h.astype(BF16)


def _in_proj(rows, x, mod_l, g, w_bf16):
    t, d = x.shape
    n = w_bf16.shape[1]
    tm = ROW_TILE
    nt = t // tm

    def cur(i):
        return jnp.minimum(i, nt - 1)

    return pl.pallas_call(
        _in_proj_kernel,
        grid=(nt + 1,),
        in_specs=[pl.BlockSpec((tm, d), lambda i: (cur(i), 0)),
                  pl.BlockSpec((1, 6, d), lambda i: (rows.mod_row(cur(i), tm), 0, 0)),
                  pl.BlockSpec((1, d), lambda i: (0, 0)),
                  pl.BlockSpec((d, n), lambda i: (0, 0))],
        out_specs=pl.BlockSpec((tm, n), lambda i: (jnp.maximum(i - 1, 0), 0)),
        out_shape=jax.ShapeDtypeStruct((t, n), F32),
        scratch_shapes=[pltpu.VMEM((2, tm, d), BF16)],
        compiler_params=_cparams(("arbitrary",), 56),
        name="in_proj",
    )(x, mod_l, g.reshape(1, d), w_bf16)


def _swap32(y):
    lane = lax.broadcasted_iota(jnp.int32, y.shape, 1)
    return jnp.where((lane & 63) < 32, pltpu.roll(y, 96, 1), pltpu.roll(y, 32, 1))


def _prep_kernel(*refs, nq, nkv, rope, emit_f32, transposed, q_scale):
    q_ref, k_ref, v_ref, qn_ref, kn_ref = refs[:5]
    pos = 5
    if rope:
        cos_ref, sin_ref = refs[pos:pos + 2]
        pos += 2
    qo_ref, ko_ref, vo_ref = refs[pos:pos + 3]
    pos += 3
    if emit_f32:
        kf_ref, vf_ref = refs[pos:pos + 2]

    def head_norm(x, gain):
        ms = jnp.mean(x * x, axis=-1, keepdims=True)
        return x * lax.rsqrt(ms + RMS_EPS) * gain

    def rot(y):
        if not rope:
            return y
        return y * cos_ref[...] + _swap32(y) * sin_ref[...]

    for h in range(nq):
        sl = slice(h * HEAD_DIM, (h + 1) * HEAD_DIM)
        y = rot(head_norm(q_ref[:, sl], qn_ref[...])) * q_scale
        qo_ref[h] = y.astype(BF16)
    for h in range(nkv):
        sl = slice(h * HEAD_DIM, (h + 1) * HEAD_DIM)
        y = head_norm(k_ref[:, sl], kn_ref[...])
        v = v_ref[:, sl]
        if emit_f32:
            kf_ref[:, sl] = y
            vf_ref[:, sl] = v
        ko_ref[h] = rot(y).astype(BF16)
        vo_ref[h] = (v.T if transposed else v).astype(BF16)


def _prep(proj, col0, nq, nkv, qn, kn, row0, n_rows, rope_tabs, emit_f32, transposed, q_scale):
    tm = 256
    rope = rope_tabs is not None
    r0 = row0 // tm
    qw, kw = nq * HEAD_DIM, nkv * HEAD_DIM
    qb, kb, vb = col0 // qw, (col0 + qw) // kw, (col0 + qw + kw) // kw
    assert col0 % qw == 0 and (col0 + qw) % kw == 0 and row0 % tm == 0
    in_specs = [pl.BlockSpec((tm, qw), lambda i: (r0 + i, qb)),
                pl.BlockSpec((tm, kw), lambda i: (r0 + i, kb)),
                pl.BlockSpec((tm, kw), lambda i: (r0 + i, vb)),
                pl.BlockSpec((1, HEAD_DIM), lambda i: (0, 0)),
                pl.BlockSpec((1, HEAD_DIM), lambda i: (0, 0))]
    args = [proj, proj, proj, qn.reshape(1, HEAD_DIM), kn.reshape(1, HEAD_DIM)]
    if rope:
        per = rope_tabs[0].shape[0] // tm
        in_specs += [pl.BlockSpec((tm, HEAD_DIM), lambda i: (i % per, 0)),
                     pl.BlockSpec((tm, HEAD_DIM), lambda i: (i % per, 0))]
        args += list(rope_tabs)
    row_major = pl.BlockSpec((nq, tm, HEAD_DIM), lambda i: (0, i, 0))
    kv_row = pl.BlockSpec((nkv, tm, HEAD_DIM), lambda i: (0, i, 0))
    kv_col = pl.BlockSpec((nkv, HEAD_DIM, tm), lambda i: (0, 0, i))
    if transposed:
        out_specs = [row_major, kv_row, kv_col]
        out_shape = [jax.ShapeDtypeStruct((nq, n_rows, HEAD_DIM), BF16),
                     jax.ShapeDtypeStruct((nkv, n_rows, HEAD_DIM), BF16),
                     jax.ShapeDtypeStruct((nkv, HEAD_DIM, n_rows), BF16)]
    else:
        out_specs = [row_major, kv_row, kv_row]
        out_shape = [jax.ShapeDtypeStruct((nq, n_rows, HEAD_DIM), BF16),
                     jax.ShapeDtypeStruct((nkv, n_rows, HEAD_DIM), BF16),
                     jax.ShapeDtypeStruct((nkv, n_rows, HEAD_DIM), BF16)]
    if emit_f32:
        out_specs += [pl.BlockSpec((tm, kw), lambda i: (i, 0)),
                      pl.BlockSpec((tm, kw), lambda i: (i, 0))]
        out_shape += [jax.ShapeDtypeStruct((n_rows, kw), F32),
                      jax.ShapeDtypeStruct((n_rows, kw), F32)]
    return pl.pallas_call(
        functools.partial(_prep_kernel, nq=nq, nkv=nkv, rope=rope, emit_f32=emit_f32,
                          transposed=transposed, q_scale=q_scale),
        grid=(n_rows // tm,),
        in_specs=in_specs,
        out_specs=out_specs,
        out_shape=out_shape,
        compiler_params=_cparams(("arbitrary",), 40),
        name="qkv_prep_rope" if rope else "qkv_prep",
    )(*args)


def _softmax_pv(s, v_list, widths, sink):
    m = jnp.max(s, axis=-1, keepdims=True)
    if sink is not None:
        m = jnp.maximum(m, sink)
    p = jnp.exp(s - m)
    l = jnp.sum(p, axis=-1, keepdims=True)
    if sink is not None:
        l = l + jnp.exp(sink - m)
    pb = p.astype(BF16)
    o = None
    c0 = 0
    for v, w in zip(v_list, widths):
        part = _dot(pb[:, c0:c0 + w], v)
        o = part if o is None else o + part
        c0 += w
    return o / l


def _ctx_attn_kernel(*refs, has_sink, nkv):
    if has_sink:
        q_ref, k_ref, v_ref, sink_ref, o_ref = refs
    else:
        q_ref, k_ref, v_ref, o_ref = refs
    for h in range(nkv):
        k = k_ref[h]
        v = v_ref[h]
        for g in range(GQA_GROUP):
            head = h * GQA_GROUP + g
            s = _dot_t(q_ref[head], k)
            sink = sink_ref[head][:, 0:1] if has_sink else None
            o = _softmax_pv(s, [v], [k.shape[0]], sink)
            o_ref[:, head * HEAD_DIM:(head + 1) * HEAD_DIM] = o.astype(BF16)


def _ctx_attention(rows, q, k, v, sink, att_cols):
    nq, nkv = q.shape[0], k.shape[0]
    s = rows.s_ctx
    has_sink = sink is not None
    in_specs = [pl.BlockSpec((nq, s, HEAD_DIM), lambda b: (0, b, 0)),
                pl.BlockSpec((nkv, s, HEAD_DIM), lambda b: (0, b, 0)),
                pl.BlockSpec((nkv, s, HEAD_DIM), lambda b: (0, b, 0))]
    args = [q, k, v]
    if has_sink:
        in_specs.append(pl.BlockSpec((nq, 1, HEAD_DIM), lambda b: (0, 0, 0)))
        args.append(sink)
    return pl.pallas_call(
        functools.partial(_ctx_attn_kernel, has_sink=has_sink, nkv=nkv),
        grid=(rows.n_ctx_seq,),
        in_specs=in_specs,
        out_specs=pl.BlockSpec((s, nq * HEAD_DIM), lambda b: (b, 0)),
        out_shape=jax.ShapeDtypeStruct((rows.t, att_cols), BF16),
        compiler_params=_cparams(("arbitrary",), 40),
        name="ctx_attention",
    )(*args)


def _window_attn_kernel(q_ref, kp_ref, kc_ref, kn_ref, vp_ref, vc_ref, vn_ref, kx_ref, vx_ref,
                        sink_ref, prev_ref, o_ref, *, nkv, n_blocks):
    del prev_ref
    n = pl.program_id(1)
    qb = WINDOW
    m_cols = GQA_GROUP * qb
    c = lax.broadcasted_iota(jnp.int32, (qb, m_cols), 0)
    r = lax.broadcasted_iota(jnp.int32, (qb, m_cols), 1) & (qb - 1)
    ok_prev = (c >= r) & (n > 0)
    ok_next = (c <= r) & (n < n_blocks - 1)
    for h in range(nkv):
        q = q_ref[h * GQA_GROUP:(h + 1) * GQA_GROUP].reshape(m_cols, HEAD_DIM)
        s = jnp.concatenate(
            [jnp.where(ok_prev, _dot_t(kp_ref[h], q), NEG_INF),
             _dot_t(kc_ref[h], q),
             jnp.where(ok_next, _dot_t(kn_ref[h], q), NEG_INF),
             _dot_t(kx_ref[0, h], q)], axis=0)
        sink = jnp.concatenate([sink_ref[h * GQA_GROUP + g] for g in range(GQA_GROUP)], axis=1)
        m = jnp.maximum(jnp.max(s, axis=0, keepdims=True), sink)
        p = jnp.exp(s - m)
        l = jnp.sum(p, axis=0, keepdims=True) + jnp.exp(sink - m)
        pb = p.astype(BF16)
        o = (_dot(vp_ref[h], pb[0:qb]) + _dot(vc_ref[h], pb[qb:2 * qb])
             + _dot(vn_ref[h], pb[2 * qb:3 * qb]) + _dot(vx_ref[0, h], pb[3 * qb:])) / l
        for g in range(GQA_GROUP):
            col = (h * GQA_GROUP + g) * HEAD_DIM
            o_ref[:, col:col + HEAD_DIM] = o[:, g * qb:(g + 1) * qb].T.astype(BF16)


def _window_attention(rows, q, k, vt, kx, vxt, sink, prev):
    nq, nkv = q.shape[0], k.shape[0]
    qb = WINDOW
    nb = rows.s_lat // qb
    out0 = rows.tc // qb
    n_ctx = kx.shape[2]

    def blk(b, n, shift):
        return b * nb + jnp.clip(n + shift, 0, nb - 1)

    def k_spec(shift):
        return pl.BlockSpec((nkv, qb, HEAD_DIM), lambda b, n: (0, blk(b, n, shift), 0))

    def vt_spec(shift):
        return pl.BlockSpec((nkv, HEAD_DIM, qb), lambda b, n: (0, 0, blk(b, n, shift)))

    return pl.pallas_call(
        functools.partial(_window_attn_kernel, nkv=nkv, n_blocks=nb),
        grid=(rows.n_lat_seq, nb),
        in_specs=[pl.BlockSpec((nq, qb, HEAD_DIM), lambda b, n: (0, b * nb + n, 0)),
                  k_spec(-1), k_spec(0), k_spec(1),
                  vt_spec(-1), vt_spec(0), vt_spec(1),
                  pl.BlockSpec((1, nkv, n_ctx, HEAD_DIM), lambda b, n: (b, 0, 0, 0)),
                  pl.BlockSpec((1, nkv, HEAD_DIM, n_ctx), lambda b, n: (b, 0, 0, 0)),
                  pl.BlockSpec((nq, 1, HEAD_DIM), lambda b, n: (0, 0, 0)),
                  pl.BlockSpec(memory_space=pl.ANY)],
        out_specs=pl.BlockSpec((qb, nq * HEAD_DIM), lambda b, n: (out0 + b * nb + n, 0)),
        out_shape=jax.ShapeDtypeStruct(prev.shape, BF16),
        input_output_aliases={10: 0},
        compiler_params=_cparams(("arbitrary", "arbitrary"), 40),
        name="window_attention",
    )(q, k, k, k, vt, vt, vt, kx, vxt, sink, prev)


def _dense_attn_kernel(q_ref, k_ref, vt_ref, kx_ref, vxt_ref, prev_ref, o_ref, *, bq, bk):
    del prev_ref
    m_cols = GQA_GROUP * bq
    q = q_ref[...].reshape(m_cols, HEAD_DIM)

    def update(k, vt, carry):
        m_old, l_old, acc = carry
        s = _dot_t(k, q)
        m_new = jnp.maximum(m_old, jnp.max(s, axis=0, keepdims=True))
        alpha = jnp.exp2(m_old - m_new)
        p = jnp.exp2(s - m_new)
        l_new = alpha * l_old + jnp.sum(p, axis=0, keepdims=True)
        acc = alpha * acc + _dot(vt, p.astype(BF16))
        return m_new, l_new, acc

    carry = (jnp.full((1, m_cols), NEG_INF, F32), jnp.zeros((1, m_cols), F32),
             jnp.zeros((HEAD_DIM, m_cols), F32))
    for c in range(k_ref.shape[1] // bk):
        carry = update(k_ref[0, c * bk:(c + 1) * bk, :], vt_ref[0, :, c * bk:(c + 1) * bk], carry)
    _, l, acc = update(kx_ref[0, 0], vxt_ref[0, 0], carry)
    o = acc / l
    for g in range(GQA_GROUP):
        o_ref[:, g * HEAD_DIM:(g + 1) * HEAD_DIM] = o[:, g * bq:(g + 1) * bq].T.astype(BF16)


def _dense_attention(rows, q, k, vt, kx, vxt, prev):
    nkv = k.shape[0]
    bq, bk = 256, 512
    s = rows.s_lat
    assert s % bk == 0 and s % bq == 0
    nqb = s // bq
    out0 = rows.tc // bq
    n_ctx = kx.shape[2]
    return pl.pallas_call(
        functools.partial(_dense_attn_kernel, bq=bq, bk=bk),
        grid=(rows.n_lat_seq, nkv, nqb),
        in_specs=[pl.BlockSpec((GQA_GROUP, bq, HEAD_DIM), lambda b, h, i: (h, b * nqb + i, 0)),
                  pl.BlockSpec((1, s, HEAD_DIM), lambda b, h, i: (h, b, 0)),
                  pl.BlockSpec((1, HEAD_DIM, s), lambda b, h, i: (h, 0, b)),
                  pl.BlockSpec((1, 1, n_ctx, HEAD_DIM), lambda b, h, i: (b, h, 0, 0)),
                  pl.BlockSpec((1, 1, HEAD_DIM, n_ctx), lambda b, h, i: (b, h, 0, 0)),
                  pl.BlockSpec(memory_space=pl.ANY)],
        out_specs=pl.BlockSpec((bq, GQA_GROUP * HEAD_DIM), lambda b, h, i: (out0 + b * nqb + i, h)),
        out_shape=jax.ShapeDtypeStruct(prev.shape, BF16),
        input_output_aliases={5: 0},
        compiler_params=_cparams(("arbitrary", "arbitrary", "arbitrary"), 48),
        name="dense_attention",
    )(q, k, vt, kx, vxt, prev)


def _dft_tables(n):
    if n % 64 != 0 or n < 128:
        j = jnp.arange(n, dtype=jnp.int32)
        ang = ((j[:, None] * j[None, :]) % n).astype(F32) * (2.0 * math.pi / n)
        scale = n ** -0.5
        return (jnp.cos(ang) * scale).astype(BF16), (jnp.sin(ang) * scale).astype(BF16)
    rows = n // 64
    k = jnp.arange(n, dtype=jnp.int32)[None, :]
    j1 = jnp.arange(rows, dtype=jnp.int32)[:, None]
    j2 = jnp.arange(64, dtype=jnp.int32)[:, None]
    a1 = ((j1 * k) % rows).astype(F32) * (2.0 * math.pi / rows)
    a2 = ((j2 * k) % n).astype(F32) * (2.0 * math.pi / n)
    scale = n ** -0.5
    c1, s1 = jnp.cos(a1)[:, None, :], jnp.sin(a1)[:, None, :]
    c2, s2 = (jnp.cos(a2) * scale)[None], (jnp.sin(a2) * scale)[None]
    cos = (c1 * c2 - s1 * s2).reshape(n, n)
    sin = (s1 * c2 + c1 * s2).reshape(n, n)
    return cos.astype(BF16), sin.astype(BF16)


def _chan_dft_kernel(a_ref, cs_ref, uc_ref, us_ref, *, gd):
    for g in range(A_GROUPS):
        sl = slice(g * gd, (g + 1) * gd)
        y = _dot(a_ref[:, sl].astype(BF16), cs_ref[...])
        uc_ref[:, sl] = y[:, :gd].astype(BF16)
        us_ref[:, sl] = y[:, gd:].astype(BF16)


def _chan_dft(proj, a_width, cs):
    t = proj.shape[0]
    gd = a_width // A_GROUPS
    tm = ROW_TILE
    return pl.pallas_call(
        functools.partial(_chan_dft_kernel, gd=gd),
        grid=(t // tm,),
        in_specs=[pl.BlockSpec((tm, a_width), lambda i: (i, 0)),
                  pl.BlockSpec((gd, 2 * gd), lambda i: (0, 0))],
        out_specs=[pl.BlockSpec((tm, a_width), lambda i: (i, 0)),
                   pl.BlockSpec((tm, a_width), lambda i: (i, 0))],
        out_shape=[jax.ShapeDtypeStruct((t, a_width), BF16)] * 2,
        compiler_params=_cparams(("arbitrary",), 40),
        name="channel_dft",
    )(proj, cs)


def _seq_dft_kernel(c_ref, s_ref, uc_ref, us_ref, *rest):
    o_ref = rest[-1]
    o_ref[...] = (_dot(c_ref[...], uc_ref[...]) - _dot(s_ref[...], us_ref[...])).astype(BF16)


def _seq_dft(uc, us, cmat, smat, row0, n_seq, prev):
    t, width = uc.shape
    s = cmat.shape[0]
    tm = min(s, ROW_TILE)
    tn = 512
    ni = s // tm
    seq0 = row0 // s
    out0 = row0 // tm
    in_specs = [pl.BlockSpec((tm, s), lambda b, j, i: (i, 0)),
                pl.BlockSpec((tm, s), lambda b, j, i: (i, 0)),
                pl.BlockSpec((s, tn), lambda b, j, i: (seq0 + b, j)),
                pl.BlockSpec((s, tn), lambda b, j, i: (seq0 + b, j))]
    args = [cmat, smat, uc, us]
    aliases = {}
    if prev is not None:
        in_specs.append(pl.BlockSpec(memory_space=pl.ANY))
        args.append(prev)
        aliases = {4: 0}
    return pl.pallas_call(
        _seq_dft_kernel,
        grid=(n_seq, width // tn, ni),
        in_specs=in_specs,
        out_specs=pl.BlockSpec((tm, tn), lambda b, j, i: (out0 + b * ni + i, j)),
        out_shape=jax.ShapeDtypeStruct((t, width), BF16),
        input_output_aliases=aliases,
        compiler_params=_cparams(("arbitrary", "arbitrary", "arbitrary"), 48),
        name="sequence_dft",
    )(*args)


def _out_proj_kernel(*refs, n_lhs):
    lhs = refs[:n_lhs]
    ws = refs[n_lhs:2 * n_lhs]
    x_ref, mod_ref, o_ref = refs[2 * n_lhs:]
    acc = None
    for a, w in zip(lhs, ws):
        part = _dot(a[...], w[...])
        acc = part if acc is None else acc + part
    o_ref[...] = x_ref[...] + mod_ref[0, 2:3, :] * acc


def _out_proj(rows, lhs_list, w_bf16, x, mod_l):
    t, d = x.shape
    tm = ROW_TILE
    in_specs, args, w_args, w_specs = [], [], [], []
    k0 = 0
    for a in lhs_list:
        kw = a.shape[1]
        in_specs.append(pl.BlockSpec((tm, kw), lambda i: (i, 0)))
        args.append(a)
        kb = k0 // kw
        assert k0 % kw == 0
        w_specs.append(pl.BlockSpec((kw, d), lambda i, kb=kb: (kb, 0)))
        w_args.append(w_bf16)
        k0 += kw
    in_specs += w_specs + [pl.BlockSpec((tm, d), lambda i: (i, 0)),
                           pl.BlockSpec((1, 6, d), lambda i: (rows.mod_row(i, tm), 0, 0))]
    args += w_args + [x, mod_l]
    return pl.pallas_call(
        functools.partial(_out_proj_kernel, n_lhs=len(lhs_list)),
        grid=(t // tm,),
        in_specs=in_specs,
        out_specs=pl.BlockSpec((tm, d), lambda i: (i, 0)),
        out_shape=jax.ShapeDtypeStruct((t, d), F32),
        compiler_params=_cparams(("arbitrary",), 56),
        name="out_proj",
    )(*args)


def _ffn_norm_kernel(x_ref, mod_ref, g_ref, whi_ref, wlo_ref, b_ref, h_ref, lg_ref):
    h = _norm_mod(x_ref[...], g_ref[...], mod_ref[0, 4:5, :], mod_ref[0, 3:4, :])
    hi = h.astype(BF16)
    lo = (h - hi.astype(F32)).astype(BF16)
    h_ref[...] = h
    lg_ref[...] = (_dot(hi, whi_ref[...]) + _dot(hi, wlo_ref[...]) + _dot(lo, whi_ref[...])
                   + b_ref[...])


def _ffn_norm_router(rows, x, mod_l, g, w_hi, w_lo, b_r):
    t, d = x.shape
    tm = ROW_TILE
    return pl.pallas_call(
        _ffn_norm_kernel,
        grid=(t // tm,),
        in_specs=[pl.BlockSpec((tm, d), lambda i: (i, 0)),
                  pl.BlockSpec((1, 6, d), lambda i: (rows.mod_row(i, tm), 0, 0)),
                  pl.BlockSpec((1, d), lambda i: (0, 0)),
                  pl.BlockSpec((d, LOGIT_LANES), lambda i: (0, 0)),
                  pl.BlockSpec((d, LOGIT_LANES), lambda i: (0, 0)),
                  pl.BlockSpec((1, LOGIT_LANES), lambda i: (0, 0))],
        out_specs=[pl.BlockSpec((tm, d), lambda i: (i, 0)),
                   pl.BlockSpec((tm, LOGIT_LANES), lambda i: (i, 0))],
        out_shape=[jax.ShapeDtypeStruct((t, d), F32),
                   jax.ShapeDtypeStruct((t, LOGIT_LANES), F32)],
        compiler_params=_cparams(("arbitrary",), 40),
        name="ffn_norm_router",
    )(x, mod_l, g.reshape(1, d), w_hi, w_lo, b_r)


def _expert_kernel(be_ref, nb_ref, src_ref, nsrc_ref, dk_ref, dt_ref, h_hbm, wg_ref, wu_ref, wd_ref,
                   y_hbm, xbuf, ybuf, wgb, wub, wdb, gsem, ssem):
    b = pl.program_id(0)
    n_used = nb_ref[0]
    bm = xbuf.shape[1]
    slot = b % 2

    def gather_copy(ref, r, dst_slot):
        return pltpu.make_async_copy(h_hbm.at[pl.ds(ref[0, 0, r], 1)], xbuf.at[dst_slot, pl.ds(r, 1)],
                                     gsem.at[dst_slot])

    def scatter_copy(r):
        return pltpu.make_async_copy(ybuf.at[pl.ds(r, 1)],
                                     y_hbm.at[dk_ref[0, 0, r], pl.ds(dt_ref[0, 0, r], 1)], ssem.at[0])

    @pl.when(b == 0)
    def _():
        for r in range(bm):
            gather_copy(src_ref, r, 0).start()

    @pl.when(b < n_used)
    def _():
        for r in range(bm):
            gather_copy(src_ref, r, slot).wait()

        @pl.when((b == 0) | (be_ref[b] != be_ref[jnp.maximum(b - 1, 0)]))
        def _():
            wgb[...] = wg_ref[0, 0].astype(BF16)
            wub[...] = wu_ref[0, 0].astype(BF16)
            wdb[...] = wd_ref[0, 0].astype(BF16)

        for r in range(bm):
            gather_copy(nsrc_ref, r, 1 - slot).start()
        x = xbuf[slot].astype(BF16)
        hid = (_silu(_dot(x, wgb[...])) * _dot(x, wub[...])).astype(BF16)

        @pl.when(b > 0)
        def _():
            for r in range(bm):
                scatter_copy(r).wait()

        ybuf[...] = _dot(hid, wdb[...])
        for r in range(bm):
            scatter_copy(r).start()

        @pl.when(b == n_used - 1)
        def _():
            for r in range(bm):
                scatter_copy(r).wait()
            for r in range(bm):
                gather_copy(nsrc_ref, r, 1 - slot).wait()


def _experts(h, src, dst_k, dst_t, blk_expert, n_used, layer, w_gate, w_up, w_down):
    t, d = h.shape
    de = w_gate.shape[3]
    bm = MOE_ROWS
    nblk = src.shape[0]

    def w_map(b, be, nb):
        return (layer, be[b], 0, 0)

    def cur(b, be, nb):
        return (b, 0, 0)

    def nxt(b, be, nb):
        return (jnp.minimum(b + 1, jnp.maximum(nb[0] - 1, 0)), 0, 0)

    def smem(imap):
        return pl.BlockSpec((1, 1, bm), imap, memory_space=pltpu.SMEM)

    grid_spec = pltpu.PrefetchScalarGridSpec(
        num_scalar_prefetch=2,
        grid=(nblk,),
        in_specs=[smem(cur), smem(nxt), smem(cur), smem(cur),
                  pl.BlockSpec(memory_space=pl.ANY),
                  pl.BlockSpec((1, 1, d, de), w_map),
                  pl.BlockSpec((1, 1, d, de), w_map),
                  pl.BlockSpec((1, 1, de, d), w_map)],
        out_specs=pl.BlockSpec(memory_space=pl.ANY),
        scratch_shapes=[pltpu.VMEM((2, bm, d), F32), pltpu.VMEM((bm, d), F32),
                        pltpu.VMEM((d, de), BF16), pltpu.VMEM((d, de), BF16),
                        pltpu.VMEM((de, d), BF16),
                        pltpu.SemaphoreType.DMA((2,)), pltpu.SemaphoreType.DMA((1,))],
    )
    return pl.pallas_call(
        _expert_kernel,
        grid_spec=grid_spec,
        out_shape=jax.ShapeDtypeStruct((TOP_K, t + bm, d), F32),
        compiler_params=_cparams(("arbitrary",), 56),
        name="moe_experts",
    )(blk_expert, n_used, src, src, dst_k, dst_t, h, w_gate, w_up, w_down)


def _combine_kernel(x_ref, y0_ref, y1_ref, w_ref, mod_ref, o_ref):
    ff = w_ref[:, 0:1] * y0_ref[0] + w_ref[:, 1:2] * y1_ref[0]
    o_ref[...] = x_ref[...] + mod_ref[0, 5:6, :] * ff


def _combine(rows, x, y, wts, mod_l):
    t, d = x.shape
    tm = 256
    return pl.pallas_call(
        _combine_kernel,
        grid=(t // tm,),
        in_specs=[pl.BlockSpec((tm, d), lambda i: (i, 0)),
                  pl.BlockSpec((1, tm, d), lambda i: (0, i, 0)),
                  pl.BlockSpec((1, tm, d), lambda i: (1, i, 0)),
                  pl.BlockSpec((tm, TOP_K), lambda i: (i, 0)),
                  pl.BlockSpec((1, 6, d), lambda i: (rows.mod_row(i, tm), 0, 0))],
        out_specs=pl.BlockSpec((tm, d), lambda i: (i, 0)),
        out_shape=jax.ShapeDtypeStruct((t, d), F32),
        compiler_params=_cparams(("arbitrary",), 40),
        name="moe_combine",
    )(x, y, y, wts, mod_l)


def _moe(rows, x, mod_l, g, layer, w_group, b_group, w_route, b_route, w_gate, w_up, w_down):
    t, d = x.shape
    n_exp = w_route.shape[1]
    per_group = n_exp // N_GROUPS
    pad = LOGIT_LANES - N_GROUPS - n_exp
    w_r = jnp.concatenate([w_group, w_route, jnp.zeros((d, pad), F32)], axis=1)
    b_r = jnp.concatenate([b_group, b_route, jnp.zeros((pad,), F32)]).reshape(1, LOGIT_LANES)
    w_hi = w_r.astype(BF16)
    w_lo = (w_r - w_hi.astype(F32)).astype(BF16)
    h, logits = _ffn_norm_router(rows, x, mod_l, g, w_hi, w_lo, b_r)

    g_logit = logits[:, :N_GROUPS]
    g_idx = jnp.argmax(g_logit, axis=-1).astype(jnp.int32)
    g_w = 1.0 / jnp.sum(jnp.exp(g_logit - jnp.max(g_logit, axis=-1, keepdims=True)), axis=-1, keepdims=True)
    e_all = logits[:, N_GROUPS:N_GROUPS + n_exp].reshape(t, N_GROUPS, per_group)
    e_logit = jnp.take_along_axis(e_all, g_idx[:, None, None], axis=1)[:, 0]
    prob = jax.nn.softmax(e_logit, axis=-1)
    lane = lax.broadcasted_iota(jnp.int32, prob.shape, 1)
    i1 = jnp.argmax(prob, axis=-1).astype(jnp.int32)[:, None]
    p1 = jnp.max(prob, axis=-1, keepdims=True)
    rest = jnp.where(lane == i1, -1.0, prob)
    i2 = jnp.argmax(rest, axis=-1).astype(jnp.int32)[:, None]
    p2 = jnp.max(rest, axis=-1, keepdims=True)
    top_p = jnp.concatenate([p1, p2], axis=1)
    top_i = jnp.concatenate([i1, i2], axis=1)
    wts = g_w * top_p / jnp.sum(top_p, axis=-1, keepdims=True)
    eid = (g_idx[:, None] * per_group + top_i).reshape(-1).astype(jnp.int32)

    bm = MOE_ROWS
    n_assign = t * TOP_K
    order = jnp.argsort(eid).astype(jnp.int32)
    e_sorted = eid[order]
    bounds = jnp.searchsorted(e_sorted, jnp.arange(n_exp + 1, dtype=jnp.int32), side='left').astype(jnp.int32)
    start, sizes = bounds[:-1], bounds[1:] - bounds[:-1]
    padded = ((sizes + bm - 1) // bm) * bm
    pad_end = jnp.cumsum(padded)
    pad_start = pad_end - padded
    nblk = -(-(n_assign + n_exp * (bm - 1)) // bm)
    blk_start = jnp.arange(nblk, dtype=jnp.int32) * bm
    blk_expert = jnp.minimum(jnp.searchsorted(pad_end, blk_start, side='right'), n_exp - 1).astype(jnp.int32)
    n_used = (pad_end[-1:] // bm).astype(jnp.int32)
    blk_off = blk_start - pad_start[blk_expert]
    n_valid = jnp.clip(sizes[blk_expert] - blk_off, 0, bm)
    r = jnp.arange(bm, dtype=jnp.int32)[None, :]
    pos = jnp.clip(start[blk_expert][:, None] + blk_off[:, None] + r, 0, n_assign - 1)
    valid = r < n_valid[:, None]
    flat = order[pos]
    shape3 = (nblk, 1, bm)
    spread = (blk_start[:, None] + r) % t
    src = jnp.where(valid, flat // TOP_K, spread).astype(jnp.int32).reshape(shape3)
    dst_k = jnp.where(valid, flat % TOP_K, 0).astype(jnp.int32).reshape(shape3)
    dst_t = jnp.where(valid, flat // TOP_K, t + r).astype(jnp.int32).reshape(shape3)
    y = _experts(h, src, dst_k, dst_t, blk_expert, n_used, layer, w_gate, w_up, w_down)
    return _combine(rows, x, y, wts, mod_l)


def _rope_tables(s_lat):
    n_rows = s_lat // GRID_W
    row = jnp.repeat(jnp.arange(n_rows, dtype=F32), GRID_W)
    col = jnp.tile(jnp.arange(GRID_W, dtype=F32), n_rows)
    axis_dim = HEAD_DIM // 2
    inv_freq = ROPE_THETA ** (-jnp.arange(0, axis_dim, 2, dtype=F32) / axis_dim)
    ar, ac = row[:, None] * inv_freq, col[:, None] * inv_freq
    cos = jnp.concatenate([jnp.cos(ar), jnp.cos(ar), jnp.cos(ac), jnp.cos(ac)], axis=1)
    sin = jnp.concatenate([-jnp.sin(ar), jnp.sin(ar), -jnp.sin(ac), jnp.sin(ac)], axis=1)
    return cos, sin


def _cache_heads(c, transposed=False):
    perm = (0, 2, 3, 1) if transposed else (0, 2, 1, 3)
    return jnp.transpose(c, perm).astype(BF16)


def kernel(x_prompt, x_sample, cache_b_k, cache_b_v, cache_c_k, cache_c_v, c, c_ctx,
           w_mod, b_mod, norm_mix, norm_ffn, w_in_even, w_out_even, q_norm_b, k_norm_b, sink_b,
           w_in_odd, w_out_odd, q_norm_c, k_norm_c, w_group, b_group, w_route, b_route,
           w_gate, w_up, w_down):
    n_ctx_seq, s_ctx, d = x_prompt.shape
    n_lat_seq, s_lat, _ = x_sample.shape
    depth = w_mod.shape[0]
    rows = _Rows(n_ctx_seq, s_ctx, n_lat_seq, s_lat)
    assert 1 + n_lat_seq <= MOD_ROWS and rows.tc % ROW_TILE == 0 and s_lat % ROW_TILE == 0
    a_width = A_GROUPS * (d // 8)
    b_heads = (d - a_width) // HEAD_DIM
    b_kv = b_heads // GQA_GROUP
    c_heads = d // HEAD_DIM
    c_kv = c_heads // GQA_GROUP
    q_scale = HEAD_DIM ** -0.5

    cvec = jnp.concatenate([c_ctx[None, :], c, jnp.zeros((MOD_ROWS - 1 - n_lat_seq, d), F32)], axis=0)
    mod = _modulation(cvec, w_mod, b_mod).reshape(depth, MOD_ROWS, 6, d)

    x = jnp.concatenate([x_prompt.reshape(rows.tc, d), x_sample.reshape(rows.tl, d)], axis=0)
    rope = _rope_tables(s_lat)
    gd = a_width // A_GROUPS
    cc, sc = _dft_tables(gd)
    chan_cs = jnp.concatenate([cc, sc], axis=1)
    dft_ctx = _dft_tables(s_ctx)
    dft_lat = _dft_tables(s_lat)

    new_kv = []
    for l in range(depth):
        i = l // 2
        mod_l = mod[l]
        if l % 2 == 0:
            proj = _in_proj(rows, x, mod_l, norm_mix[l], w_in_even[i].astype(BF16))
            qc, kc, vc, kf, vf = _prep(proj, a_width, b_heads, b_kv, q_norm_b[i], k_norm_b[i],
                                       0, rows.tc, None, True, False, q_scale)
            q, k, vt = _prep(proj, a_width, b_heads, b_kv, q_norm_b[i], k_norm_b[i],
                             rows.tc, rows.tl, rope, False, True, q_scale)
            sink = jnp.broadcast_to(sink_b[i][:, None, None], (b_heads, 1, HEAD_DIM))
            att = _ctx_attention(rows, qc, kc, vc, sink, b_heads * HEAD_DIM)
            att = _window_attention(rows, q, k, vt, _cache_heads(cache_b_k[:, i]),
                                    _cache_heads(cache_b_v[:, i], True), sink, att)
            uc, us = _chan_dft(proj, a_width, chan_cs)
            fm = _seq_dft(uc, us, dft_ctx[0], dft_ctx[1], 0, n_ctx_seq, None)
            fm = _seq_dft(uc, us, dft_lat[0], dft_lat[1], rows.tc, n_lat_seq, fm)
            x = _out_proj(rows, [fm, att], w_out_even[i].astype(BF16), x, mod_l)
            nkv = b_kv
        else:
            proj = _in_proj(rows, x, mod_l, norm_mix[l], w_in_odd[i].astype(BF16))
            qc, kc, vc, kf, vf = _prep(proj, 0, c_heads, c_kv, q_norm_c[i], k_norm_c[i],
                                       0, rows.tc, None, True, False, q_scale)
            q, k, vt = _prep(proj, 0, c_heads, c_kv, q_norm_c[i], k_norm_c[i],
                             rows.tc, rows.tl, rope, False, True, q_scale * LOG2E)
            att = _ctx_attention(rows, qc, kc, vc, None, c_heads * HEAD_DIM)
            att = _dense_attention(rows, q, k, vt, _cache_heads(cache_c_k[:, i]),
                                   _cache_heads(cache_c_v[:, i], True), att)
            x = _out_proj(rows, [att], w_out_odd[i].astype(BF16), x, mod_l)
            nkv = c_kv
        new_kv.append((kf.reshape(n_ctx_seq, s_ctx, nkv, HEAD_DIM),
                       vf.reshape(n_ctx_seq, s_ctx, nkv, HEAD_DIM)))
        x = _moe(rows, x, mod_l, norm_ffn[l], l, w_group[l], b_group[l], w_route[l], b_route[l],
                 w_gate, w_up, w_down)

    y_prompt = x[:rows.tc].reshape(n_ctx_seq, s_ctx, d)
    y_sample = x[rows.tc:].reshape(n_lat_seq, s_lat, d)
    new_b_k = jnp.stack([new_kv[l][0] for l in range(0, depth, 2)], axis=1)
    new_b_v = jnp.stack([new_kv[l][1] for l in range(0, depth, 2)], axis=1)
    new_c_k = jnp.stack([new_kv[l][0] for l in range(1, depth, 2)], axis=1)
    new_c_v = jnp.stack([new_kv[l][1] for l in range(1, depth, 2)], axis=1)
    return (y_prompt, y_sample, new_b_k, new_b_v, new_c_k, new_c_v)
```
